```python
import math
import jax, jax.numpy as jnp
from jax import lax
import numpy as np

D_MODEL = 1024
BATCH = 2
SEQ = 8192
DEPTH = 4

N_EVEN = (DEPTH + 1) // 2
N_ODD = DEPTH // 2
D_FF = 2816
RMS_EPS = 1e-6
LN_EPS = 1e-5
SC_WIDTH = D_MODEL // 2
SC_KERNEL = 3
CM_WIDTH = D_MODEL // 2
CM_KERNEL = 31
IN_AB = 3 * SC_WIDTH + 2 * CM_WIDTH
N_HEADS = 16
HEAD_DIM = D_MODEL // N_HEADS
ATTN_WIDTH = N_HEADS * HEAD_DIM
DILATED_BRANCHES = ((128, 1), (512, 4), (2048, 16))
ATTN_BLOCK = 128
NEG_INF = -1e30

kernel_name = "hybrid_shortconv_conformer_dilated_macaron"


def rms_norm(x, g):
    xf = x.astype(jnp.float32)
    y = xf * lax.rsqrt(jnp.mean(xf * xf, axis=-1, keepdims=True) + RMS_EPS)
    return (y * g.astype(jnp.float32)).astype(x.dtype)


def layer_norm(x, g, b):
    xf = x.astype(jnp.float32)
    mu = jnp.mean(xf, axis=-1, keepdims=True)
    xc = xf - mu
    y = xc * lax.rsqrt(jnp.mean(xc * xc, axis=-1, keepdims=True) + LN_EPS)
    return (y * g.astype(jnp.float32) + b.astype(jnp.float32)).astype(x.dtype)


def swiglu_ffn(x, w_gate_up, w_down):
    gate, up = jnp.split(x @ w_gate_up, 2, axis=-1)
    return (jax.nn.silu(gate) * up) @ w_down


def causal_depthwise_conv(x, w):
    k_len, ch = w.shape
    return lax.conv_general_dilated(
        x, w[:, None, :].astype(x.dtype), window_strides=(1,), padding=[(k_len - 1, 0)],
        dimension_numbers=('NWC', 'WIO', 'NWC'), feature_group_count=ch)


def conv_mixers(h, w_in, a_kernel, b_kernel, b_bias, b_ln_g, b_ln_b, w_out):
    z = h @ w_in
    a_b, a_c, a_x, b_val, b_gate = jnp.split(
        z, [SC_WIDTH, 2 * SC_WIDTH, 3 * SC_WIDTH, 3 * SC_WIDTH + CM_WIDTH], axis=-1)
    y_a = a_b * causal_depthwise_conv(a_c * a_x, a_kernel)
    u = b_val * jax.nn.sigmoid(b_gate)
    u = causal_depthwise_conv(u, b_kernel) + b_bias.astype(u.dtype)
    y_b = jax.nn.silu(layer_norm(u, b_ln_g, b_ln_b))
    return jnp.concatenate([y_a, y_b], axis=-1) @ w_out


def alibi_slopes(n_heads):
    return np.array([2.0 ** (-8.0 * (i + 1) / n_heads) for i in range(n_heads)], dtype=np.float32)


def dilated_branch(q, k, v, slopes, window, dil):
    bsz, seq, n_h, d_h = q.shape
    span = window // dil
    assert span <= ATTN_BLOCK
    n_sub = seq // dil
    n_pad = -(-n_sub // ATTN_BLOCK) * ATTN_BLOCK
    n_blk = n_pad // ATTN_BLOCK

    def to_blocks(t):
        t = t.reshape(bsz, n_sub, dil, n_h, d_h).transpose(0, 2, 1, 3, 4)
        t = jnp.pad(t, ((0, 0), (0, 0), (0, n_pad - n_sub), (0, 0), (0, 0)))
        return t.reshape(bsz, dil, n_blk, ATTN_BLOCK, n_h, d_h)

    def with_prev(t):
        prev = jnp.pad(t, ((0, 0), (0, 0), (1, 0), (0, 0), (0, 0), (0, 0)))[:, :, :-1]
        return jnp.concatenate([prev, t], axis=3)

    qb = to_blocks(q.astype(jnp.float32)) * (1.0 / math.sqrt(d_h))
    kk = with_prev(to_blocks(k.astype(jnp.float32)))
    vv = with_prev(to_blocks(v.astype(jnp.float32)))

    scores = jnp.einsum('brnqhd,brnkhd->brnhqk', qb, kk)
    q_pos = jnp.arange(ATTN_BLOCK) + ATTN_BLOCK
    k_pos = jnp.arange(2 * ATTN_BLOCK)
    rel = q_pos[:, None] - k_pos[None, :]
    key_abs = jnp.arange(n_blk)[:, None] * ATTN_BLOCK - ATTN_BLOCK + k_pos[None, :]
    valid = ((rel >= 0) & (rel <= span))[None] & (key_abs >= 0)[:, None, :]
    bias = -slopes[:, None, None] * (dil * rel).astype(jnp.float32)[None]
    scores = jnp.where(valid[None, None, :, None], scores + bias, NEG_INF)
    lse = jax.nn.logsumexp(scores, axis=-1)
    p = jnp.exp(scores - lse[..., None])
    out = jnp.einsum('brnhqk,brnkhd->brnqhd', p, vv)

    out = out.reshape(bsz, dil, n_pad, n_h, d_h)[:, :, :n_sub]
    out = out.transpose(0, 2, 1, 3, 4).reshape(bsz, seq, n_h, d_h)
    lse = lse.transpose(0, 1, 2, 4, 3).reshape(bsz, dil, n_pad, n_h)[:, :, :n_sub]
    lse = lse.transpose(0, 2, 1, 3).reshape(bsz, seq, n_h)
    return out, lse


def dilated_attention(h, w_qkv, w_o):
    bsz, seq, _ = h.shape
    qkv = (h @ w_qkv).reshape(bsz, seq, 3, N_HEADS, HEAD_DIM)
    q, k, v = qkv[:, :, 0], qkv[:, :, 1], qkv[:, :, 2]
    slopes = jnp.asarray(alibi_slopes(N_HEADS))
    branches = [dilated_branch(q, k, v, slopes, w, d) for (w, d) in DILATED_BRANCHES]
    outs = jnp.stack([o for o, _ in branches], axis=0)
    lses = jnp.stack([l for _, l in branches], axis=0)
    alpha = jax.nn.softmax(lses, axis=0)
    o = jnp.einsum('gbsh,gbshd->bshd', alpha, outs)
    return o.reshape(bsz, seq, ATTN_WIDTH).astype(h.dtype) @ w_o


def setup_inputs(seed: int = 0) -> dict:
    key = jax.random.key(seed)
    ks = jax.random.split(key, 24)
    f32 = jnp.float32

    def dense(k, shape, fan_in):
        return jax.random.normal(k, shape, f32) * (fan_in ** -0.5)

    def gain(k, shape):
        return 1.0 + 0.01 * jax.random.normal(k, shape, f32)

    def small(k, shape):
        return 0.02 * jax.random.normal(k, shape, f32)

    return {
        'x': jax.random.normal(ks[0], (BATCH, SEQ, D_MODEL), f32),
        'ffn1_norm': gain(ks[1], (DEPTH, D_MODEL)),
        'ffn1_w_gate_up': dense(ks[2], (DEPTH, D_MODEL, 2 * D_FF), D_MODEL),
        'ffn1_w_down': dense(ks[3], (DEPTH, D_FF, D_MODEL), D_FF),
        'mix_norm': gain(ks[4], (DEPTH, D_MODEL)),
        'ffn2_norm': gain(ks[5], (DEPTH, D_MODEL)),
        'ffn2_w_gate_up': dense(ks[6], (DEPTH, D_MODEL, 2 * D_FF), D_MODEL),
        'ffn2_w_down': dense(ks[7], (DEPTH, D_FF, D_MODEL), D_FF),
        'conv_w_in': dense(ks[8], (N_EVEN, D_MODEL, IN_AB), D_MODEL),
        'conv_a_kernel': dense(ks[9], (N_EVEN, SC_KERNEL, SC_WIDTH), SC_KERNEL),
        'conv_b_kernel': dense(ks[10], (N_EVEN, CM_KERNEL, CM_WIDTH), CM_KERNEL),
        'conv_b_bias': small(ks[11], (N_EVEN, CM_WIDTH)),
        'conv_b_ln_gain': gain(ks[12], (N_EVEN, CM_WIDTH)),
        'conv_b_ln_bias': small(ks[13], (N_EVEN, CM_WIDTH)),
        'conv_w_out': dense(ks[14], (N_EVEN, SC_WIDTH + CM_WIDTH, D_MODEL), SC_WIDTH + CM_WIDTH),
        'attn_w_qkv': dense(ks[15], (N_ODD, D_MODEL, 3 * ATTN_WIDTH), D_MODEL),
        'attn_w_o': dense(ks[16], (N_ODD, ATTN_WIDTH, D_MODEL), ATTN_WIDTH),
        'final_norm': gain(ks[17], (D_MODEL,)),
    }


def reference(x, ffn1_norm, ffn1_w_gate_up, ffn1_w_down, mix_norm, ffn2_norm, ffn2_w_gate_up,
              ffn2_w_down, conv_w_in, conv_a_kernel, conv_b_kernel, conv_b_bias, conv_b_ln_gain,
              conv_b_ln_bias, conv_w_out, attn_w_qkv, attn_w_o, final_norm):
    for layer in range(DEPTH):
        x = x + 0.5 * swiglu_ffn(rms_norm(x, ffn1_norm[layer]), ffn1_w_gate_up[layer], ffn1_w_down[layer])
        h = rms_norm(x, mix_norm[layer])
        i = layer // 2
        if layer % 2 == 0:
            x = x + conv_mixers(h, conv_w_in[i], conv_a_kernel[i], conv_b_kernel[i], conv_b_bias[i],
                                conv_b_ln_gain[i], conv_b_ln_bias[i], conv_w_out[i])
        else:
            x = x + dilated_attention(h, attn_w_qkv[i], attn_w_o[i])
        x = x + 0.5 * swiglu_ffn(rms_norm(x, ffn2_norm[layer]), ffn2_w_gate_up[layer], ffn2_w_down[layer])
    return rms_norm(x, final_norm)
```

```python
import functools
import math

import numpy as np
import jax
import jax.numpy as jnp
from jax import lax
from jax.experimental import pallas as pl
from jax.experimental.pallas import tpu as pltpu

D_MODEL = 1024
DEPTH = 4
D_FF = 2816
RMS_EPS = 1e-6
LN_EPS = 1e-5
SC_WIDTH = 512
SC_KERNEL = 3
CM_WIDTH = 512
CM_KERNEL = 31
IN_AB = 3 * SC_WIDTH + 2 * CM_WIDTH
N_HEADS = 16
HEAD_DIM = 64
ATTN_WIDTH = N_HEADS * HEAD_DIM
DILATED_BRANCHES = ((128, 1), (512, 4), (2048, 16))
ATTN_BLOCK = 128
NEG_INF = -1e30

V7X_LANES = 128
V7X_SUBLANES = 8
V7X_MXU_COLS = 256
V7X_VMEM_BYTES = 64 * 1024 * 1024

F32 = jnp.float32
BF16 = jnp.bfloat16

TOKEN_TILE = 512
FF_CHUNK = V7X_MXU_COLS
CONV_ROWS = 64
CONV_A_HALO = V7X_SUBLANES
CONV_B_HALO = 32
HEADS_PER_STEP = V7X_LANES // HEAD_DIM


def _vmem_limit(estimate_bytes):
    return int(min(V7X_VMEM_BYTES - 8 * 1024 * 1024, estimate_bytes))


def _rms_norm(x, g):
    return x * lax.rsqrt(jnp.mean(x * x, axis=-1, keepdims=True) + RMS_EPS) * g


def _sigmoid(x):
    return 1.0 / (1.0 + jnp.exp(-x))


def _resident(shape):
    zeros = (0,) * len(shape)
    return pl.BlockSpec(shape, lambda *_: zeros, pipeline_mode=pl.Buffered(1))


def _ffn_kernel(x_ref, g_ref, wgu_ref, wd_ref, gf_ref, o_ref, a_scr, *, final_norm):
    x = x_ref[...]
    h = _rms_norm(x, g_ref[...]).astype(BF16)
    for c in range(D_FF // FF_CHUNK):
        lo = c * FF_CHUNK
        gate = jnp.dot(h, wgu_ref[:, lo:lo + FF_CHUNK], preferred_element_type=F32)
        up = jnp.dot(h, wgu_ref[:, D_FF + lo:D_FF + lo + FF_CHUNK], preferred_element_type=F32)
        a_scr[:, lo:lo + FF_CHUNK] = (gate * _sigmoid(gate) * up).astype(BF16)
    y = x + 0.5 * jnp.dot(a_scr[...], wd_ref[...], preferred_element_type=F32)
    if final_norm:
        y = _rms_norm(y, gf_ref[...])
    o_ref[...] = y


def _ffn(x2d, g, wgu, wd, g_final, *, final_norm):
    n_tok = x2d.shape[0]
    tm = TOKEN_TILE
    est = (4 * tm * D_MODEL * 4
           + (wgu.size + wd.size) * 2
           + tm * D_FF * 2
           + 6 * tm * D_MODEL * 4)
    return pl.pallas_call(
        functools.partial(_ffn_kernel, final_norm=final_norm),
        out_shape=jax.ShapeDtypeStruct((n_tok, D_MODEL), F32),
        grid=(n_tok // tm,),
        in_specs=[
            pl.BlockSpec((tm, D_MODEL), lambda i: (i, 0)),
            _resident((1, D_MODEL)),
            _resident((D_MODEL, 2 * D_FF)),
            _resident((D_FF, D_MODEL)),
            _resident((1, D_MODEL)),
        ],
        out_specs=pl.BlockSpec((tm, D_MODEL), lambda i: (i, 0)),
        scratch_shapes=[pltpu.VMEM((tm, D_FF), BF16)],
        compiler_params=pltpu.CompilerParams(
            dimension_semantics=("arbitrary",), vmem_limit_bytes=_vmem_limit(est)),
        name="ffn_final" if final_norm else "ffn",
    )(x2d, g, wgu, wd, g_final)


def _norm_proj_kernel(x_ref, g_ref, w_ref, o_ref):
    h = _rms_norm(x_ref[...], g_ref[...]).astype(BF16)
    o_ref[...] = jnp.dot(h, w_ref[...], preferred_element_type=F32)


def _norm_proj(x2d, g, w):
    n_tok = x2d.shape[0]
    n_out = w.shape[1]
    tm = TOKEN_TILE
    est = (2 * tm * D_MODEL * 4 + 2 * tm * n_out * 4 + w.size * 2 + 2 * tm * n_out * 4)
    return pl.pallas_call(
        _norm_proj_kernel,
        out_shape=jax.ShapeDtypeStruct((n_tok, n_out), F32),
        grid=(n_tok // tm,),
        in_specs=[
            pl.BlockSpec((tm, D_MODEL), lambda i: (i, 0)),
            _resident((1, D_MODEL)),
            _resident(w.shape),
        ],
        out_specs=pl.BlockSpec((tm, n_out), lambda i: (i, 0)),
        compiler_params=pltpu.CompilerParams(
            dimension_semantics=("arbitrary",), vmem_limit_bytes=_vmem_limit(est)),
        name="norm_proj",
    )(x2d, g, w)


def _proj_residual_kernel(x_ref, y_ref, w_ref, o_ref):
    o_ref[...] = x_ref[...] + jnp.dot(
        y_ref[...].astype(BF16), w_ref[...], preferred_element_type=F32)


def _proj_residual(x2d, y2d, w):
    n_tok = x2d.shape[0]
    tm = TOKEN_TILE
    est = 8 * tm * D_MODEL * 4 + w.size * 2
    return pl.pallas_call(
        _proj_residual_kernel,
        out_shape=jax.ShapeDtypeStruct((n_tok, D_MODEL), F32),
        grid=(n_tok // tm,),
        in_specs=[
            pl.BlockSpec((tm, D_MODEL), lambda i: (i, 0)),
            pl.BlockSpec((tm, y2d.shape[1]), lambda i: (i, 0)),
            _resident(w.shape),
        ],
        out_specs=pl.BlockSpec((tm, D_MODEL), lambda i: (i, 0)),
        compiler_params=pltpu.CompilerParams(
            dimension_semantics=("arbitrary",), vmem_limit_bytes=_vmem_limit(est)),
        name="proj_residual",
    )(x2d, y2d, w)


def _conv_mixer_kernel(x_ref, g_ref, w_in_ref, ka_ref, kb_ref, bb_ref, lng_ref, lnb_ref,
                       w_out_ref, o_ref, cx_scr, u_scr, cv_scr, y_scr, *, tm):
    ha, hb = CONV_A_HALO, CONV_B_HALO

    @pl.when(pl.program_id(1) == 0)
    def _start_of_sequence():
        cx_scr[0:ha, :] = jnp.zeros((ha, SC_WIDTH), F32)
        u_scr[0:hb, :] = jnp.zeros((hb, CM_WIDTH), F32)

    x = x_ref[0]
    h = _rms_norm(x, g_ref[...]).astype(BF16)

    def proj(j):
        return jnp.dot(h, w_in_ref[:, j * SC_WIDTH:(j + 1) * SC_WIDTH],
                       preferred_element_type=F32)

    a_b = proj(0)
    cx_scr[ha:ha + tm, :] = proj(1) * proj(2)
    u_scr[hb:hb + tm, :] = proj(3) * _sigmoid(proj(4))

    conv_a = ka_ref[0:1, :] * cx_scr[ha - 2:ha - 2 + tm, :]
    conv_a = conv_a + ka_ref[1:2, :] * cx_scr[ha - 1:ha - 1 + tm, :]
    conv_a = conv_a + ka_ref[2:3, :] * cx_scr[ha:ha + tm, :]
    y_scr[:, 0:SC_WIDTH] = (a_b * conv_a).astype(BF16)

    first_off = hb - (CM_KERNEL - 1)

    def conv_rows(i, carry):
        r0 = pl.multiple_of(i * CONV_ROWS, CONV_ROWS)
        for j in range(CM_WIDTH // V7X_LANES):
            lanes = slice(j * V7X_LANES, (j + 1) * V7X_LANES)
            window = u_scr[pl.ds(r0, CONV_ROWS + hb), lanes]
            acc = jnp.zeros((CONV_ROWS, V7X_LANES), F32)
            for sub in range(V7X_SUBLANES):
                offs = [o for o in range(first_off, hb + 1) if o % V7X_SUBLANES == sub]
                span = max(offs) - sub + CONV_ROWS
                shifted = window[sub:sub + span, :]
                for o in offs:
                    k = o - first_off
                    acc = acc + kb_ref[k:k + 1, lanes] * shifted[o - sub:o - sub + CONV_ROWS, :]
            cv_scr[pl.ds(r0, CONV_ROWS), lanes] = acc
        return carry

    lax.fori_loop(0, tm // CONV_ROWS, conv_rows, 0)

    u = cv_scr[...] + bb_ref[...]
    mu = jnp.mean(u, axis=-1, keepdims=True)
    uc = u - mu
    un = uc * lax.rsqrt(jnp.mean(uc * uc, axis=-1, keepdims=True) + LN_EPS)
    un = un * lng_ref[...] + lnb_ref[...]
    y_scr[:, SC_WIDTH:SC_WIDTH + CM_WIDTH] = (un * _sigmoid(un)).astype(BF16)

    o_ref[0] = x + jnp.dot(y_scr[...], w_out_ref[...], preferred_element_type=F32)

    cx_scr[0:ha, :] = cx_scr[tm:tm + ha, :]
    u_scr[0:hb, :] = u_scr[tm:tm + hb, :]


def _conv_mixer(x, g, w_in, ka, kb, bb, lng, lnb, w_out):
    bsz, seq, _ = x.shape
    tm = TOKEN_TILE
    est = (4 * tm * D_MODEL * 4 + (w_in.size + w_out.size) * 2
           + (3 * tm + 64) * SC_WIDTH * 4 + tm * D_MODEL * 2
           + 8 * tm * D_MODEL * 4)
    tile = pl.BlockSpec((1, tm, D_MODEL), lambda b, t: (b, t, 0))
    return pl.pallas_call(
        functools.partial(_conv_mixer_kernel, tm=tm),
        out_shape=jax.ShapeDtypeStruct(x.shape, F32),
        grid=(bsz, seq // tm),
        in_specs=[
            tile,
            _resident((1, D_MODEL)),
            _resident(w_in.shape),
            _resident(ka.shape),
            _resident(kb.shape),
            _resident((1, CM_WIDTH)),
            _resident((1, CM_WIDTH)),
            _resident((1, CM_WIDTH)),
            _resident(w_out.shape),
        ],
        out_specs=tile,
        scratch_shapes=[
            pltpu.VMEM((CONV_A_HALO + tm, SC_WIDTH), F32),
            pltpu.VMEM((CONV_B_HALO + tm, CM_WIDTH), F32),
            pltpu.VMEM((tm, CM_WIDTH), F32),
            pltpu.VMEM((tm, SC_WIDTH + CM_WIDTH), BF16),
        ],
        compiler_params=pltpu.CompilerParams(
            dimension_semantics=("arbitrary", "arbitrary"),
            vmem_limit_bytes=_vmem_limit(est)),
        name="conv_mixer",
    )(x, g, w_in, ka, kb, bb, lng, lnb, w_out)


def _attention_kernel(slopes_ref, q_ref, k_ref, v_ref, o_ref, lse_scr, *, seq):
    blk = ATTN_BLOCK
    pair = pl.program_id(1)
    slope = [slopes_ref[HEADS_PER_STEP * pair + h] for h in range(HEADS_PER_STEP)]
    first_head = lax.broadcasted_iota(jnp.int32, (1, V7X_LANES), 1) < HEAD_DIM
    nt_dims = (((1,), (1,)), ((), ()))

    def softmax_block(q_rows, k_rows, bias):
        q = q_ref[0, q_rows, :] * (1.0 / math.sqrt(HEAD_DIM))
        kb = k_ref[0, k_rows, :].astype(BF16)
        vb = v_ref[0, k_rows, :].astype(BF16)
        outs, lses = [], []
        for h in range(HEADS_PER_STEP):
            mine = first_head if h == 0 else jnp.logical_not(first_head)
            qh = jnp.where(mine, q, 0.0).astype(BF16)
            s = lax.dot_general(qh, kb, nt_dims, preferred_element_type=F32) + bias[h]
            m = jnp.max(s, axis=-1, keepdims=True)
            p = jnp.exp(s - m)
            l = jnp.sum(p, axis=-1, keepdims=True)
            pv = jnp.dot(p.astype(BF16), vb, preferred_element_type=F32)
            outs.append(pv / l)
            lses.append(jnp.broadcast_to(m + jnp.log(l), (blk, V7X_LANES)))
        return jnp.where(first_head, outs[0], outs[1]), jnp.where(first_head, lses[0], lses[1])

    def emit(q_rows, out, lse, first_branch):
        if first_branch:
            o_ref[0, q_rows, :] = out
            lse_scr[q_rows, :] = lse
        else:
            o_old = o_ref[0, q_rows, :]
            lse_old = lse_scr[q_rows, :]
            top = jnp.maximum(lse_old, lse)
            w_old = jnp.exp(lse_old - top)
            w_new = jnp.exp(lse - top)
            den = w_old + w_new
            o_ref[0, q_rows, :] = (o_old * w_old + out * w_new) / den
            lse_scr[q_rows, :] = top + jnp.log(den)

    iq = lax.broadcasted_iota(jnp.int32, (blk, 2 * blk), 0)
    ik = lax.broadcasted_iota(jnp.int32, (blk, 2 * blk), 1)
    rel = blk + iq - ik

    for branch, (window, dil) in enumerate(DILATED_BRANCHES):
        span = window // dil
        assert span <= blk
        n_blk = seq // dil // blk
        valid = (rel >= 0) & (rel <= span)
        dist = (dil * rel).astype(F32)
        bias_two = [jnp.where(valid, -slope[h] * dist, NEG_INF) for h in range(HEADS_PER_STEP)]
        bias_one = [b[:, blk:] for b in bias_two]

        def rows(start, size, dil=dil):
            if dil == 1:
                return pl.ds(start, size)
            return pl.ds(start, size, stride=dil)

        def residue_class(r, carry, dil=dil, n_blk=n_blk, rows=rows, branch=branch,
                          bias_one=bias_one, bias_two=bias_two):
            q_rows = rows(r, blk)
            out, lse = softmax_block(q_rows, q_rows, bias_one)
            emit(q_rows, out, lse, branch == 0)

            def block(n, c):
                q_start = r + dil * blk * n
                q_rows = rows(q_start, blk)
                k_rows = rows(q_start - dil * blk, 2 * blk)
                out, lse = softmax_block(q_rows, k_rows, bias_two)
                emit(q_rows, out, lse, branch == 0)
                return c

            lax.fori_loop(1, n_blk, block, 0)
            return carry

        lax.fori_loop(0, dil, residue_class, 0)


def _attention(slopes, qkv):
    bsz, seq, _ = qkv.shape
    n_pairs = N_HEADS // HEADS_PER_STEP
    est = (2 * 4 * seq * V7X_LANES * 4
           + seq * V7X_LANES * 4
           + 4 * 1024 * 1024)

    def cols(offset):
        return pl.BlockSpec((1, seq, V7X_LANES), lambda b, p: (b, 0, offset + p))

    return pl.pallas_call(
        functools.partial(_attention_kernel, seq=seq),
        out_shape=jax.ShapeDtypeStruct((bsz, seq, ATTN_WIDTH), F32),
        grid=(bsz, n_pairs),
        in_specs=[
            pl.BlockSpec(memory_space=pltpu.SMEM),
            cols(0), cols(n_pairs), cols(2 * n_pairs),
        ],
        out_specs=cols(0),
        scratch_shapes=[pltpu.VMEM((seq, V7X_LANES), F32)],
        compiler_params=pltpu.CompilerParams(
            dimension_semantics=("arbitrary", "arbitrary"),
            vmem_limit_bytes=_vmem_limit(est)),
        name="dilated_attention",
    )(slopes, qkv, qkv, qkv)


def _alibi_slopes():
    return np.array([2.0 ** (-8.0 * (i + 1) / N_HEADS) for i in range(N_HEADS)], dtype=np.float32)


def kernel(x, ffn1_norm, ffn1_w_gate_up, ffn1_w_down, mix_norm, ffn2_norm, ffn2_w_gate_up,
           ffn2_w_down, conv_w_in, conv_a_kernel, conv_b_kernel, conv_b_bias, conv_b_ln_gain,
           conv_b_ln_bias, conv_w_out, attn_w_qkv, attn_w_o, final_norm):
    bsz, seq, d = x.shape
    n_tok = bsz * seq
    row = lambda v: v.reshape(1, -1)
    slopes = jnp.asarray(_alibi_slopes())
    g_final = row(final_norm)

    for layer in range(DEPTH):
        x2d = _ffn(x.reshape(n_tok, d), row(ffn1_norm[layer]),
                   ffn1_w_gate_up[layer].astype(BF16), ffn1_w_down[layer].astype(BF16),
                   g_final, final_norm=False)
        i = layer // 2
        if layer % 2 == 0:
            x2d = _conv_mixer(
                x2d.reshape(bsz, seq, d), row(mix_norm[layer]), conv_w_in[i].astype(BF16),
                conv_a_kernel[i], conv_b_kernel[i], row(conv_b_bias[i]),
                row(conv_b_ln_gain[i]), row(conv_b_ln_bias[i]),
                conv_w_out[i].astype(BF16)).reshape(n_tok, d)
        else:
            qkv = _norm_proj(x2d, row(mix_norm[layer]), attn_w_qkv[i].astype(BF16))
            o = _attention(slopes, qkv.reshape(bsz, seq, 3 * ATTN_WIDTH))
            x2d = _proj_residual(x2d, o.reshape(n_tok, ATTN_WIDTH), attn_w_o[i].astype(BF16))
        x = _ffn(x2d, row(ffn2_norm[layer]),
                 ffn2_w_gate_up[layer].astype(BF16), ffn2_w_down[layer].astype(BF16),
                 g_final, final_norm=(layer == DEPTH - 1)).reshape(bsz, seq, d)
    return x
```

```python
import functools
import math

import numpy as np
import jax
import jax.numpy as jnp
from jax import lax
from jax.experimental import pallas as pl
from jax.experimental.pallas import tpu as pltpu

D_MODEL = 1024
DEPTH = 4
D_FF = 2816
RMS_EPS = 1e-6
LN_EPS = 1e-5
SC_WIDTH = 512
SC_KERNEL = 3
CM_WIDTH = 512
CM_KERNEL = 31
IN_AB = 3 * SC_WIDTH + 2 * CM_WIDTH
N_HEADS = 16
HEAD_DIM = 64
ATTN_WIDTH = N_HEADS * HEAD_DIM
DILATED_BRANCHES = ((128, 1), (512, 4), (2048, 16))
ATTN_BLOCK = 128
NEG_INF = -1e30

V7X_LANES = 128
V7X_SUBLANES = 8
V7X_MXU_COLS = 256
V7X_VMEM_BYTES = 64 * 1024 * 1024

F32 = jnp.float32
BF16 = jnp.bfloat16

TOKEN_TILE = 512
FF_CHUNK = V7X_MXU_COLS
CONV_ROWS = 64
CONV_A_HALO = V7X_SUBLANES
CONV_B_HALO = 32
HEADS_PER_STEP = V7X_LANES // HEAD_DIM
ATTN_GROUP = 4
ATTN_CLASSES = 4
LOG2_E = math.log2(math.e)


def _vmem_limit(estimate_bytes):
    return int(min(V7X_VMEM_BYTES - 8 * 1024 * 1024, estimate_bytes))


def _rms_norm(x, g):
    return x * lax.rsqrt(jnp.mean(x * x, axis=-1, keepdims=True) + RMS_EPS) * g


def _sigmoid(x):
    return 1.0 / (1.0 + jnp.exp(-x))


def _resident(shape):
    zeros = (0,) * len(shape)
    return pl.BlockSpec(shape, lambda *_: zeros, pipeline_mode=pl.Buffered(1))


def _ffn_kernel(x_ref, g_ref, wgu_ref, wd_ref, gf_ref, o_ref, a_scr, *, final_norm):
    x = x_ref[...]
    h = _rms_norm(x, g_ref[...]).astype(BF16)
    for c in range(D_FF // FF_CHUNK):
        lo = c * FF_CHUNK
        gate = jnp.dot(h, wgu_ref[:, lo:lo + FF_CHUNK], preferred_element_type=F32)
        up = jnp.dot(h, wgu_ref[:, D_FF + lo:D_FF + lo + FF_CHUNK], preferred_element_type=F32)
        a_scr[:, lo:lo + FF_CHUNK] = (gate * _sigmoid(gate) * up).astype(BF16)
    y = x + 0.5 * jnp.dot(a_scr[...], wd_ref[...], preferred_element_type=F32)
    if final_norm:
        y = _rms_norm(y, gf_ref[...])
    o_ref[...] = y


def _ffn(x2d, g, wgu, wd, g_final, *, final_norm):
    n_tok = x2d.shape[0]
    tm = TOKEN_TILE
    est = (4 * tm * D_MODEL * 4
           + (wgu.size + wd.size) * 2
           + tm * D_FF * 2
           + 6 * tm * D_MODEL * 4)
    return pl.pallas_call(
        functools.partial(_ffn_kernel, final_norm=final_norm),
        out_shape=jax.ShapeDtypeStruct((n_tok, D_MODEL), F32),
        grid=(n_tok // tm,),
        in_specs=[
            pl.BlockSpec((tm, D_MODEL), lambda i: (i, 0)),
            _resident((1, D_MODEL)),
            _resident((D_MODEL, 2 * D_FF)),
            _resident((D_FF, D_MODEL)),
            _resident((1, D_MODEL)),
        ],
        out_specs=pl.BlockSpec((tm, D_MODEL), lambda i: (i, 0)),
        scratch_shapes=[pltpu.VMEM((tm, D_FF), BF16)],
        compiler_params=pltpu.CompilerParams(
            dimension_semantics=("arbitrary",), vmem_limit_bytes=_vmem_limit(est)),
        name="ffn_final" if final_norm else "ffn",
    )(x2d, g, wgu, wd, g_final)


def _norm_proj_kernel(x_ref, g_ref, w_ref, o_ref):
    h = _rms_norm(x_ref[0], g_ref[...]).astype(BF16)
    o_ref[0, 0] = jnp.dot(h, w_ref[...], preferred_element_type=F32)


def _norm_proj(x, g, w):
    bsz, seq, _ = x.shape
    n_out = w.shape[1]
    rows = seq // ATTN_CLASSES
    tm = TOKEN_TILE
    est = (2 * tm * D_MODEL * 4 + 2 * tm * n_out * 4 + w.size * 2 + 2 * tm * n_out * 4)
    return pl.pallas_call(
        _norm_proj_kernel,
        out_shape=jax.ShapeDtypeStruct((ATTN_CLASSES, bsz, rows, n_out), F32),
        grid=(bsz, ATTN_CLASSES, rows // tm),
        in_specs=[
            pl.BlockSpec((1, tm, D_MODEL), lambda b, c, i: (b, i, c)),
            _resident((1, D_MODEL)),
            _resident(w.shape),
        ],
        out_specs=pl.BlockSpec((1, 1, tm, n_out), lambda b, c, i: (c, b, i, 0)),
        compiler_params=pltpu.CompilerParams(
            dimension_semantics=("arbitrary",) * 3, vmem_limit_bytes=_vmem_limit(est)),
        name="norm_proj",
    )(x.reshape(bsz, rows, ATTN_CLASSES * D_MODEL), g, w)


def _proj_residual_kernel(x_ref, y_ref, w_ref, o_ref):
    o_ref[0] = x_ref[0] + jnp.dot(
        y_ref[0, 0].astype(BF16), w_ref[...], preferred_element_type=F32)


def _proj_residual(x, y, w):
    bsz, seq, _ = x.shape
    rows = seq // ATTN_CLASSES
    tm = TOKEN_TILE
    est = 8 * tm * D_MODEL * 4 + w.size * 2
    x_tile = pl.BlockSpec((1, tm, D_MODEL), lambda b, c, i: (b, i, c))
    return pl.pallas_call(
        _proj_residual_kernel,
        out_shape=jax.ShapeDtypeStruct((bsz, rows, ATTN_CLASSES * D_MODEL), F32),
        grid=(bsz, ATTN_CLASSES, rows // tm),
        in_specs=[
            x_tile,
            pl.BlockSpec((1, 1, tm, y.shape[-1]), lambda b, c, i: (c, b, i, 0)),
            _resident(w.shape),
        ],
        out_specs=x_tile,
        compiler_params=pltpu.CompilerParams(
            dimension_semantics=("arbitrary",) * 3, vmem_limit_bytes=_vmem_limit(est)),
        name="proj_residual",
    )(x.reshape(bsz, rows, ATTN_CLASSES * D_MODEL), y, w).reshape(bsz, seq, D_MODEL)


def _conv_mixer_kernel(x_ref, g_ref, w_in_ref, ka_ref, kb_ref, bb_ref, lng_ref, lnb_ref,
                       w_out_ref, o_ref, cx_scr, u_scr, cv_scr, y_scr, *, tm):
    ha, hb = CONV_A_HALO, CONV_B_HALO

    @pl.when(pl.program_id(1) == 0)
    def _start_of_sequence():
        cx_scr[0:ha, :] = jnp.zeros((ha, SC_WIDTH), F32)
        u_scr[0:hb, :] = jnp.zeros((hb, CM_WIDTH), F32)

    x = x_ref[0]
    h = _rms_norm(x, g_ref[...]).astype(BF16)

    def proj(j):
        return jnp.dot(h, w_in_ref[:, j * SC_WIDTH:(j + 1) * SC_WIDTH],
                       preferred_element_type=F32)

    a_b = proj(0)
    cx_scr[ha:ha + tm, :] = proj(1) * proj(2)
    u_scr[hb:hb + tm, :] = proj(3) * _sigmoid(proj(4))

    conv_a = ka_ref[0:1, :] * cx_scr[ha - 2:ha - 2 + tm, :]
    conv_a = conv_a + ka_ref[1:2, :] * cx_scr[ha - 1:ha - 1 + tm, :]
    conv_a = conv_a + ka_ref[2:3, :] * cx_scr[ha:ha + tm, :]
    y_scr[:, 0:SC_WIDTH] = (a_b * conv_a).astype(BF16)

    first_off = hb - (CM_KERNEL - 1)

    def conv_rows(i, carry):
        r0 = pl.multiple_of(i * CONV_ROWS, CONV_ROWS)
        for j in range(CM_WIDTH // V7X_LANES):
            lanes = slice(j * V7X_LANES, (j + 1) * V7X_LANES)
            window = u_scr[pl.ds(r0, CONV_ROWS + hb), lanes]
            acc = jnp.zeros((CONV_ROWS, V7X_LANES), F32)
            for sub in range(V7X_SUBLANES):
                offs = [o for o in range(first_off, hb + 1) if o % V7X_SUBLANES == sub]
                span = max(offs) - sub + CONV_ROWS
                shifted = window[sub:sub + span, :]
                for o in offs:
                    k = o - first_off
                    acc = acc + kb_ref[k:k + 1, lanes] * shifted[o - sub:o - sub + CONV_ROWS, :]
            cv_scr[pl.ds(r0, CONV_ROWS), lanes] = acc
        return carry

    lax.fori_loop(0, tm // CONV_ROWS, conv_rows, 0)

    u = cv_scr[...] + bb_ref[...]
    mu = jnp.mean(u, axis=-1, keepdims=True)
    uc = u - mu
    un = uc * lax.rsqrt(jnp.mean(uc * uc, axis=-1, keepdims=True) + LN_EPS)
    un = un * lng_ref[...] + lnb_ref[...]
    y_scr[:, SC_WIDTH:SC_WIDTH + CM_WIDTH] = (un * _sigmoid(un)).astype(BF16)

    o_ref[0] = x + jnp.dot(y_scr[...], w_out_ref[...], preferred_element_type=F32)

    cx_scr[0:ha, :] = cx_scr[tm:tm + ha, :]
    u_scr[0:hb, :] = u_scr[tm:tm + hb, :]


def _conv_mixer(x, g, w_in, ka, kb, bb, lng, lnb, w_out):
    bsz, seq, _ = x.shape
    tm = TOKEN_TILE
    est = (4 * tm * D_MODEL * 4 + (w_in.size + w_out.size) * 2
           + (3 * tm + 64) * SC_WIDTH * 4 + tm * D_MODEL * 2
           + 8 * tm * D_MODEL * 4)
    tile = pl.BlockSpec((1, tm, D_MODEL), lambda b, t: (b, t, 0))
    return pl.pallas_call(
        functools.partial(_conv_mixer_kernel, tm=tm),
        out_shape=jax.ShapeDtypeStruct(x.shape, F32),
        grid=(bsz, seq // tm),
        in_specs=[
            tile,
            _resident((1, D_MODEL)),
            _resident(w_in.shape),
            _resident(ka.shape),
            _resident(kb.shape),
            _resident((1, CM_WIDTH)),
            _resident((1, CM_WIDTH)),
            _resident((1, CM_WIDTH)),
            _resident(w_out.shape),
        ],
        out_specs=tile,
        scratch_shapes=[
            pltpu.VMEM((CONV_A_HALO + tm, SC_WIDTH), F32),
            pltpu.VMEM((CONV_B_HALO + tm, CM_WIDTH), F32),
            pltpu.VMEM((tm, CM_WIDTH), F32),
            pltpu.VMEM((tm, SC_WIDTH + CM_WIDTH), BF16),
        ],
        compiler_params=pltpu.CompilerParams(
            dimension_semantics=("arbitrary", "arbitrary"),
            vmem_limit_bytes=_vmem_limit(est)),
        name="conv_mixer",
    )(x, g, w_in, ka, kb, bb, lng, lnb, w_out)


def _attention_kernel(slopes_ref, q_ref, k_ref, v_ref, o_ref, m_scr, l_scr, *, seq):
    blk, grp, ncls = ATTN_BLOCK, ATTN_GROUP, ATTN_CLASSES
    sub = blk // ncls
    pair = pl.program_id(1)
    slope = [slopes_ref[HEADS_PER_STEP * pair + h] for h in range(HEADS_PER_STEP)]
    first_head = lax.broadcasted_iota(jnp.int32, (1, V7X_LANES), 1) < HEAD_DIM
    head_lanes = [first_head, jnp.logical_not(first_head)]
    nt_dims = (((1,), (1,)), ((), ()))
    q_scale = LOG2_E / math.sqrt(HEAD_DIM)

    q_cls = [q_ref.at[c, 0] for c in range(ncls)]
    k_cls = [k_ref.at[c, 0] for c in range(ncls)]
    v_cls = [v_ref.at[c, 0] for c in range(ncls)]
    o_cls = [o_ref.at[c, 0] for c in range(ncls)]
    m_cls = [m_scr.at[c] for c in range(ncls)]
    l_cls = [l_scr.at[c] for c in range(ncls)]

    def class_rows(c, j):
        def load(refs, first, n):
            start = pl.multiple_of((j * grp + first) * blk, blk)
            return refs[c][pl.ds(start, n * blk), :]

        def store(refs, value):
            refs[c][pl.ds(pl.multiple_of(j * grp * blk, blk), grp * blk), :] = value
        return load, store

    def subclass_rows(c, a, step):
        def load(refs, first, n):
            assert first == 0 and n == grp
            return refs[c][pl.ds(a, grp * blk, stride=step), :]

        def store(refs, value):
            refs[c][pl.ds(a, grp * blk, stride=step), :] = value
        return load, store

    def natural_rows(j):
        def load(refs, first, n):
            start = pl.multiple_of((j * grp + first) * sub, sub)
            per_class = [refs[c][pl.ds(start, n * sub), :] for c in range(ncls)]
            return jnp.concatenate(
                [per_class[c][b * sub:(b + 1) * sub] for b in range(n) for c in range(ncls)], axis=0)

        def store(refs, value):
            start = pl.multiple_of(j * grp * sub, sub)
            for c in range(ncls):
                refs[c][pl.ds(start, grp * sub), :] = jnp.concatenate(
                    [value[b * blk + c * sub:b * blk + (c + 1) * sub] for b in range(grp)], axis=0)
        return load, store

    def run_group(access, first_group, bias_one, bias_two, first_branch, last_branch):
        load, store = access
        q = load(q_cls, 0, grp) * q_scale
        k_first, k_blocks = (0, grp) if first_group else (-1, grp + 1)
        kb = load(k_cls, k_first, k_blocks).astype(BF16)
        vb = load(v_cls, k_first, k_blocks).astype(BF16)
        vb = jnp.concatenate([vb, jnp.ones(vb.shape, BF16)], axis=1)
        qh = [jnp.where(head_lanes[h], q, 0.0).astype(BF16) for h in range(HEADS_PER_STEP)]

        ms, ls, pvs = [], [], []
        for u in range(grp):
            if first_group and u == 0:
                keys, bias = slice(0, blk), bias_one
            else:
                lo = u - 1 if first_group else u
                keys, bias = slice(lo * blk, (lo + 2) * blk), bias_two
            rows = slice(u * blk, (u + 1) * blk)
            lhs = jnp.concatenate([qh[h][rows] for h in range(HEADS_PER_STEP)], axis=0)
            s = lax.dot_general(lhs, kb[keys], nt_dims, preferred_element_type=F32) + bias
            m = jnp.max(s, axis=-1, keepdims=True)
            p = jnp.exp2(s - m).astype(BF16)
            r = jnp.dot(p, vb[keys], preferred_element_type=F32)
            mb = jnp.broadcast_to(m, (HEADS_PER_STEP * blk, V7X_LANES))
            ms.append(jnp.where(first_head, mb[:blk], mb[blk:]))
            pvs.append(jnp.where(first_head, r[:blk, :V7X_LANES], r[blk:, :V7X_LANES]))
            ls.append(jnp.where(first_head, r[:blk, V7X_LANES:], r[blk:, V7X_LANES:]))
        m_new = jnp.concatenate(ms, axis=0)
        l_new = jnp.concatenate(ls, axis=0)
        acc = jnp.concatenate(pvs, axis=0)

        if not first_branch:
            m_old, l_old, acc_old = load(m_cls, 0, grp), load(l_cls, 0, grp), load(o_cls, 0, grp)
            top = jnp.maximum(m_old, m_new)
            w_old = jnp.exp2(m_old - top)
            w_new = jnp.exp2(m_new - top)
            l_new = w_old * l_old + w_new * l_new
            acc = w_old * acc_old + w_new * acc
            m_new = top
        if last_branch:
            store(o_cls, acc / l_new)
        else:
            store(m_cls, m_new)
            store(l_cls, l_new)
            store(o_cls, acc)

    def bias_tables(dil, natural):
        def position(idx):
            if not natural:
                return idx
            inside = idx & (blk - 1)
            return (idx - inside) + ncls * (inside & (sub - 1)) + (inside >> int(math.log2(sub)))
        iq = position(lax.broadcasted_iota(jnp.int32, (blk, 2 * blk), 0))
        ik = position(lax.broadcasted_iota(jnp.int32, (blk, 2 * blk), 1))
        rel = blk + iq - ik
        valid = (rel >= 0) & (rel <= ATTN_BLOCK)
        dist = (dil * rel).astype(F32)
        two = jnp.concatenate(
            [jnp.where(valid, (-LOG2_E * slope[h]) * dist, NEG_INF) for h in range(HEADS_PER_STEP)],
            axis=0)
        return two[:, blk:], two

    rows_per_class = seq // ncls
    for branch, (window, dil) in enumerate(DILATED_BRANCHES):
        assert window // dil == ATTN_BLOCK
        first_branch, last_branch = branch == 0, branch == len(DILATED_BRANCHES) - 1
        bias_one, bias_two = bias_tables(dil, natural=(dil == 1))
        run = functools.partial(run_group, bias_one=bias_one, bias_two=bias_two,
                                first_branch=first_branch, last_branch=last_branch)

        def sweep(make_access, n_groups, run=run):
            run(make_access(0), True)
            if n_groups > 1:
                def later(j, carry):
                    run(make_access(j), False)
                    return carry
                lax.fori_loop(1, n_groups, later, 0)

        if dil == 1:
            sweep(natural_rows, seq // (grp * blk))
        elif dil == ncls:
            for c in range(ncls):
                sweep(functools.partial(class_rows, c), rows_per_class // (grp * blk))
        else:
            step = dil // ncls
            assert rows_per_class // step == grp * blk
            for c in range(ncls):
                def subclass(a, carry, c=c, step=step, run=run):
                    run(subclass_rows(c, a, step), True)
                    return carry
                lax.fori_loop(0, step, subclass, 0)


def _attention(slopes, qkv):
    ncls, bsz, rows, _ = qkv.shape
    n_pairs = N_HEADS // HEADS_PER_STEP
    block_bytes = ncls * rows * V7X_LANES * 4
    est = (2 * 4 * block_bytes
           + 2 * block_bytes
           + 6 * 1024 * 1024)

    def cols(offset):
        return pl.BlockSpec((ncls, 1, rows, V7X_LANES), lambda b, p: (0, b, 0, offset + p))

    return pl.pallas_call(
        functools.partial(_attention_kernel, seq=ncls * rows),
        out_shape=jax.ShapeDtypeStruct((ncls, bsz, rows, ATTN_WIDTH), F32),
        grid=(bsz, n_pairs),
        in_specs=[
            pl.BlockSpec(memory_space=pltpu.SMEM),
            cols(0), cols(n_pairs), cols(2 * n_pairs),
        ],
        out_specs=cols(0),
        scratch_shapes=[pltpu.VMEM((ncls, rows, V7X_LANES), F32),
                        pltpu.VMEM((ncls, rows, V7X_LANES), F32)],
        compiler_params=pltpu.CompilerParams(
            dimension_semantics=("arbitrary", "arbitrary"),
            vmem_limit_bytes=_vmem_limit(est)),
        name="dilated_attention",
    )(slopes, qkv, qkv, qkv)


def _alibi_slopes():
    return np.array([2.0 ** (-8.0 * (i + 1) / N_HEADS) for i in range(N_HEADS)], dtype=np.float32)


def kernel(x, ffn1_norm, ffn1_w_gate_up, ffn1_w_down, mix_norm, ffn2_norm, ffn2_w_gate_up,
           ffn2_w_down, conv_w_in, conv_a_kernel, conv_b_kernel, conv_b_bias, conv_b_ln_gain,
           conv_b_ln_bias, conv_w_out, attn_w_qkv, attn_w_o, final_norm):
    bsz, seq, d = x.shape
    n_tok = bsz * seq
    row = lambda v: v.reshape(1, -1)
    slopes = jnp.asarray(_alibi_slopes())
    g_final = row(final_norm)

    for layer in range(DEPTH):
        x = _ffn(x.reshape(n_tok, d), row(ffn1_norm[layer]),
                 ffn1_w_gate_up[layer].astype(BF16), ffn1_w_down[layer].astype(BF16),
                 g_final, final_norm=False).reshape(bsz, seq, d)
        i = layer // 2
        if layer % 2 == 0:
            x = _conv_mixer(
                x, row(mix_norm[layer]), conv_w_in[i].astype(BF16),
                conv_a_kernel[i], conv_b_kernel[i], row(conv_b_bias[i]),
                row(conv_b_ln_gain[i]), row(conv_b_ln_bias[i]), conv_w_out[i].astype(BF16))
        else:
            qkv = _norm_proj(x, row(mix_norm[layer]), attn_w_qkv[i].astype(BF16))
            o = _attention(slopes, qkv)
            x = _proj_residual(x, o, attn_w_o[i].astype(BF16))
        x = _ffn(x.reshape(n_tok, d), row(ffn2_norm[layer]),
                 ffn2_w_gate_up[layer].astype(BF16), ffn2_w_down[layer].astype(BF16),
                 g_final, final_norm=(layer == DEPTH - 1)).reshape(bsz, seq, d)
    return x
```

```python
import functools
import math

import numpy as np
import jax
import jax.numpy as jnp
from jax import lax
from jax.experimental import pallas as pl
from jax.experimental.pallas import tpu as pltpu

D_MODEL = 1024
DEPTH = 4
D_FF = 2816
RMS_EPS = 1e-6
LN_EPS = 1e-5
SC_WIDTH = 512
SC_KERNEL = 3
CM_WIDTH = 512
CM_KERNEL = 31
IN_AB = 3 * SC_WIDTH + 2 * CM_WIDTH
N_HEADS = 16
HEAD_DIM = 64
ATTN_WIDTH = N_HEADS * HEAD_DIM
DILATED_BRANCHES = ((128, 1), (512, 4), (2048, 16))
ATTN_BLOCK = 128
NEG_INF = -1e30

V7X_LANES = 128
V7X_SUBLANES = 8
V7X_MXU_COLS = 256
V7X_VMEM_BYTES = 64 * 1024 * 1024

F32 = jnp.float32
BF16 = jnp.bfloat16

TOKEN_TILE = 512
FF_CHUNK = V7X_MXU_COLS
CONV_ROWS = 64
CONV_A_HALO = V7X_SUBLANES
CONV_B_HALO = 32
HEADS_PER_STEP = V7X_LANES // HEAD_DIM
ATTN_GROUP = 4
ATTN_CLASSES = 4
ATTN_UNROLL = {1: 5, 4: 3, 16: 2}
LOG2_E = math.log2(math.e)


def _vmem_limit(estimate_bytes):
    return int(min(V7X_VMEM_BYTES - 8 * 1024 * 1024, estimate_bytes))


def _rms_norm(x, g):
    return x * lax.rsqrt(jnp.mean(x * x, axis=-1, keepdims=True) + RMS_EPS) * g


def _sigmoid(x):
    return 1.0 / (1.0 + jnp.exp(-x))


def _resident(shape, layer=None):
    zeros = (0,) * len(shape)
    if layer is None:
        return pl.BlockSpec(shape, lambda *_: zeros, pipeline_mode=pl.Buffered(1))
    return pl.BlockSpec((None,) + tuple(shape), lambda *_: (layer,) + zeros,
                        pipeline_mode=pl.Buffered(1))


def _ffn_kernel(x_ref, g_ref, wgu_ref, wd_ref, gf_ref, o_ref, a_scr, *, final_norm):
    x = x_ref[...]
    h = _rms_norm(x, g_ref[...]).astype(BF16)
    for c in range(D_FF // FF_CHUNK):
        lo = c * FF_CHUNK
        gate = jnp.dot(h, wgu_ref[:, lo:lo + FF_CHUNK], preferred_element_type=F32)
        up = jnp.dot(h, wgu_ref[:, D_FF + lo:D_FF + lo + FF_CHUNK], preferred_element_type=F32)
        a_scr[:, lo:lo + FF_CHUNK] = (gate * _sigmoid(gate) * up).astype(BF16)
    y = x + 0.5 * jnp.dot(a_scr[...], wd_ref[...], preferred_element_type=F32)
    if final_norm:
        y = _rms_norm(y, gf_ref[...])
    o_ref[...] = y


def _ffn(x2d, g, wgu, wd, g_final, *, layer, final_norm):
    n_tok = x2d.shape[0]
    tm = TOKEN_TILE
    est = (4 * tm * D_MODEL * 4
           + 3 * D_MODEL * D_FF * 2
           + tm * D_FF * 2
           + 6 * tm * D_MODEL * 4)
    return pl.pallas_call(
        functools.partial(_ffn_kernel, final_norm=final_norm),
        out_shape=jax.ShapeDtypeStruct((n_tok, D_MODEL), F32),
        grid=(n_tok // tm,),
        in_specs=[
            pl.BlockSpec((tm, D_MODEL), lambda i: (i, 0)),
            _resident((1, D_MODEL), layer),
            _resident((D_MODEL, 2 * D_FF), layer),
            _resident((D_FF, D_MODEL), layer),
            _resident((1, D_MODEL)),
        ],
        out_specs=pl.BlockSpec((tm, D_MODEL), lambda i: (i, 0)),
        scratch_shapes=[pltpu.VMEM((tm, D_FF), BF16)],
        compiler_params=pltpu.CompilerParams(
            dimension_semantics=("arbitrary",), vmem_limit_bytes=_vmem_limit(est)),
        name="ffn_final" if final_norm else "ffn",
    )(x2d, g, wgu, wd, g_final)


def _class_rows(c, tm):
    return pl.ds(c, tm // ATTN_CLASSES, stride=ATTN_CLASSES)


def _lane_block(j):
    return slice(j * V7X_LANES, (j + 1) * V7X_LANES)


def _to_class_major(tile, scr, tm):
    n_blocks = tile.shape[1] // V7X_LANES
    for j in range(n_blocks):
        scr[j] = tile[:, _lane_block(j)]
    return jnp.concatenate(
        [jnp.concatenate([scr[j, _class_rows(c, tm), :] for j in range(n_blocks)], axis=1)
         for c in range(ATTN_CLASSES)], axis=0)


def _from_class_major(tile, scr, tm):
    n_blocks = tile.shape[1] // V7X_LANES
    per_class = tm // ATTN_CLASSES
    for j in range(n_blocks):
        for c in range(ATTN_CLASSES):
            scr[j, _class_rows(c, tm), :] = tile[c * per_class:(c + 1) * per_class, _lane_block(j)]
    return jnp.concatenate([scr[j] for j in range(n_blocks)], axis=1)


def _norm_proj_kernel(x_ref, g_ref, w_ref, o_ref, scr, *, tm):
    per_class = tm // ATTN_CLASSES
    h = _rms_norm(_to_class_major(x_ref[0], scr, tm), g_ref[...]).astype(BF16)
    y = jnp.dot(h, w_ref[...], preferred_element_type=F32)
    for c in range(ATTN_CLASSES):
        o_ref[c, 0] = y[c * per_class:(c + 1) * per_class]


def _norm_proj(x, g, w, *, layer, w_layer):
    bsz, seq, _ = x.shape
    n_out = w.shape[-1]
    tm = TOKEN_TILE
    per_class = tm // ATTN_CLASSES
    est = (2 * tm * D_MODEL * 4 + 2 * tm * n_out * 4 + D_MODEL * n_out * 2 + 2 * tm * n_out * 4)
    return pl.pallas_call(
        functools.partial(_norm_proj_kernel, tm=tm),
        out_shape=jax.ShapeDtypeStruct((ATTN_CLASSES, bsz, seq // ATTN_CLASSES, n_out), F32),
        grid=(bsz, seq // tm),
        in_specs=[
            pl.BlockSpec((1, tm, D_MODEL), lambda b, i: (b, i, 0)),
            _resident((1, D_MODEL), layer),
            _resident(w.shape[1:], w_layer),
        ],
        out_specs=pl.BlockSpec((ATTN_CLASSES, 1, per_class, n_out), lambda b, i: (0, b, i, 0)),
        scratch_shapes=[pltpu.VMEM((D_MODEL // V7X_LANES, tm, V7X_LANES), F32)],
        compiler_params=pltpu.CompilerParams(
            dimension_semantics=("arbitrary",) * 2, vmem_limit_bytes=_vmem_limit(est)),
        name="norm_proj",
    )(x, g, w)


def _proj_residual_kernel(x_ref, y_ref, w_ref, o_ref, scr, *, tm):
    y = jnp.concatenate([y_ref[c, 0] for c in range(ATTN_CLASSES)], axis=0).astype(BF16)
    r = jnp.dot(y, w_ref[...], preferred_element_type=F32)
    o_ref[0] = x_ref[0] + _from_class_major(r, scr, tm)


def _proj_residual(x, y, w, *, w_layer):
    bsz, seq, _ = x.shape
    tm = TOKEN_TILE
    per_class = tm // ATTN_CLASSES
    est = 8 * tm * D_MODEL * 4 + ATTN_WIDTH * D_MODEL * 2
    x_tile = pl.BlockSpec((1, tm, D_MODEL), lambda b, i: (b, i, 0))
    return pl.pallas_call(
        functools.partial(_proj_residual_kernel, tm=tm),
        out_shape=jax.ShapeDtypeStruct(x.shape, F32),
        grid=(bsz, seq // tm),
        in_specs=[
            x_tile,
            pl.BlockSpec((ATTN_CLASSES, 1, per_class, y.shape[-1]), lambda b, i: (0, b, i, 0)),
            _resident(w.shape[1:], w_layer),
        ],
        out_specs=x_tile,
        scratch_shapes=[pltpu.VMEM((D_MODEL // V7X_LANES, tm, V7X_LANES), F32)],
        compiler_params=pltpu.CompilerParams(
            dimension_semantics=("arbitrary",) * 2, vmem_limit_bytes=_vmem_limit(est)),
        name="proj_residual",
    )(x, y, w)


def _conv_mixer_kernel(x_ref, g_ref, w_in_ref, ka_ref, kb_ref, bb_ref, lng_ref, lnb_ref,
                       w_out_ref, o_ref, cx_scr, u_scr, cv_scr, y_scr, *, tm):
    ha, hb = CONV_A_HALO, CONV_B_HALO

    @pl.when(pl.program_id(1) == 0)
    def _start_of_sequence():
        cx_scr[0:ha, :] = jnp.zeros((ha, SC_WIDTH), F32)
        u_scr[0:hb, :] = jnp.zeros((hb, CM_WIDTH), F32)

    x = x_ref[0]
    h = _rms_norm(x, g_ref[...]).astype(BF16)

    def proj(j):
        return jnp.dot(h, w_in_ref[:, j * SC_WIDTH:(j + 1) * SC_WIDTH],
                       preferred_element_type=F32)

    a_b = proj(0)
    cx_scr[ha:ha + tm, :] = proj(1) * proj(2)
    u_scr[hb:hb + tm, :] = proj(3) * _sigmoid(proj(4))

    conv_a = ka_ref[0:1, :] * cx_scr[ha - 2:ha - 2 + tm, :]
    conv_a = conv_a + ka_ref[1:2, :] * cx_scr[ha - 1:ha - 1 + tm, :]
    conv_a = conv_a + ka_ref[2:3, :] * cx_scr[ha:ha + tm, :]
    y_scr[:, 0:SC_WIDTH] = (a_b * conv_a).astype(BF16)

    first_off = hb - (CM_KERNEL - 1)

    def conv_rows(i, carry):
        r0 = pl.multiple_of(i * CONV_ROWS, CONV_ROWS)
        for j in range(CM_WIDTH // V7X_LANES):
            lanes = slice(j * V7X_LANES, (j + 1) * V7X_LANES)
            n_win = CONV_ROWS + hb
            window = u_scr[pl.ds(r0, n_win), lanes]
            acc = jnp.zeros((CONV_ROWS, V7X_LANES), F32)
            for sub in range(V7X_SUBLANES):
                offs = [o for o in range(first_off, hb + 1) if o % V7X_SUBLANES == sub]
                shifted = pltpu.roll(window, n_win - sub, axis=0) if sub else window
                for o in offs:
                    k = o - first_off
                    acc = acc + kb_ref[k:k + 1, lanes] * shifted[o - sub:o - sub + CONV_ROWS, :]
            cv_scr[pl.ds(r0, CONV_ROWS), lanes] = acc
        return carry

    lax.fori_loop(0, tm // CONV_ROWS, conv_rows, 0)

    u = cv_scr[...] + bb_ref[...]
    mu = jnp.mean(u, axis=-1, keepdims=True)
    uc = u - mu
    un = uc * lax.rsqrt(jnp.mean(uc * uc, axis=-1, keepdims=True) + LN_EPS)
    un = un * lng_ref[...] + lnb_ref[...]
    y_scr[:, SC_WIDTH:SC_WIDTH + CM_WIDTH] = (un * _sigmoid(un)).astype(BF16)

    o_ref[0] = x + jnp.dot(y_scr[...], w_out_ref[...], preferred_element_type=F32)

    cx_scr[0:ha, :] = cx_scr[tm:tm + ha, :]
    u_scr[0:hb, :] = u_scr[tm:tm + hb, :]


def _conv_mixer(x, g, w_in, ka, kb, bb, lng, lnb, w_out, *, layer, w_layer):
    bsz, seq, _ = x.shape
    tm = TOKEN_TILE
    est = (4 * tm * D_MODEL * 4 + (D_MODEL * IN_AB + D_MODEL * D_MODEL) * 2
           + (3 * tm + 64) * SC_WIDTH * 4 + tm * D_MODEL * 2
           + 8 * tm * D_MODEL * 4)
    tile = pl.BlockSpec((1, tm, D_MODEL), lambda b, t: (b, t, 0))
    return pl.pallas_call(
        functools.partial(_conv_mixer_kernel, tm=tm),
        out_shape=jax.ShapeDtypeStruct(x.shape, F32),
        grid=(bsz, seq // tm),
        in_specs=[
            tile,
            _resident((1, D_MODEL), layer),
            _resident(w_in.shape[1:], w_layer),
            _resident(ka.shape[1:], w_layer),
            _resident(kb.shape[1:], w_layer),
            _resident((1, CM_WIDTH), w_layer),
            _resident((1, CM_WIDTH), w_layer),
            _resident((1, CM_WIDTH), w_layer),
            _resident(w_out.shape[1:], w_layer),
        ],
        out_specs=tile,
        scratch_shapes=[
            pltpu.VMEM((CONV_A_HALO + tm, SC_WIDTH), F32),
            pltpu.VMEM((CONV_B_HALO + tm, CM_WIDTH), F32),
            pltpu.VMEM((tm, CM_WIDTH), F32),
            pltpu.VMEM((tm, SC_WIDTH + CM_WIDTH), BF16),
        ],
        compiler_params=pltpu.CompilerParams(
            dimension_semantics=("arbitrary", "arbitrary"),
            vmem_limit_bytes=_vmem_limit(est)),
        name="conv_mixer",
    )(x, g, w_in, ka, kb, bb, lng, lnb, w_out)


def _attention_kernel(slopes_ref, q_ref, k_ref, v_ref, o_ref, m_scr, l_scr, *, seq):
    blk, grp, ncls = ATTN_BLOCK, ATTN_GROUP, ATTN_CLASSES
    sub = blk // ncls
    pair = pl.program_id(1)
    slope = [slopes_ref[HEADS_PER_STEP * pair + h] for h in range(HEADS_PER_STEP)]
    first_head = lax.broadcasted_iota(jnp.int32, (1, V7X_LANES), 1) < HEAD_DIM
    head_lanes = [first_head, jnp.logical_not(first_head)]
    nt_dims = (((1,), (1,)), ((), ()))
    q_scale = LOG2_E / math.sqrt(HEAD_DIM)

    q_cls = [q_ref.at[c, 0] for c in range(ncls)]
    k_cls = [k_ref.at[c, 0] for c in range(ncls)]
    v_cls = [v_ref.at[c, 0] for c in range(ncls)]
    o_cls = [o_ref.at[c, 0] for c in range(ncls)]
    m_cls = [m_scr.at[c] for c in range(ncls)]
    l_cls = [l_scr.at[c] for c in range(ncls)]

    def class_rows(c, j):
        def load(refs, first, n):
            start = pl.multiple_of((j * grp + first) * blk, blk)
            return refs[c][pl.ds(start, n * blk), :]

        def store(refs, value):
            refs[c][pl.ds(pl.multiple_of(j * grp * blk, blk), grp * blk), :] = value
        return load, store

    def subclass_rows(c, a, step):
        def load(refs, first, n):
            assert first == 0 and n == grp
            return refs[c][pl.ds(a, grp * blk, stride=step), :]

        def store(refs, value):
            refs[c][pl.ds(a, grp * blk, stride=step), :] = value
        return load, store

    def natural_rows(j):
        def load(refs, first, n):
            start = pl.multiple_of((j * grp + first) * sub, sub)
            per_class = [refs[c][pl.ds(start, n * sub), :] for c in range(ncls)]
            return jnp.concatenate(
                [per_class[c][b * sub:(b + 1) * sub] for b in range(n) for c in range(ncls)], axis=0)

        def store(refs, value):
            start = pl.multiple_of(j * grp * sub, sub)
            for c in range(ncls):
                refs[c][pl.ds(start, grp * sub), :] = jnp.concatenate(
                    [value[b * blk + c * sub:b * blk + (c + 1) * sub] for b in range(grp)], axis=0)
        return load, store

    def run_group(access, first_group, bias_one, bias_two, first_branch, last_branch):
        load, store = access
        q = load(q_cls, 0, grp) * q_scale
        k_first, k_blocks = (0, grp) if first_group else (-1, grp + 1)
        kb = load(k_cls, k_first, k_blocks).astype(BF16)
        vb = load(v_cls, k_first, k_blocks).astype(BF16)
        vb = jnp.concatenate([vb, jnp.ones(vb.shape, BF16)], axis=1)
        qh = [jnp.where(head_lanes[h], q, 0.0).astype(BF16) for h in range(HEADS_PER_STEP)]

        ms, ls, pvs = [], [], []
        for u in range(grp):
            if first_group and u == 0:
                keys, bias = slice(0, blk), bias_one
            else:
                lo = u - 1 if first_group else u
                keys, bias = slice(lo * blk, (lo + 2) * blk), bias_two
            rows = slice(u * blk, (u + 1) * blk)
            lhs = jnp.concatenate([qh[h][rows] for h in range(HEADS_PER_STEP)], axis=0)
            s = lax.dot_general(lhs, kb[keys], nt_dims, preferred_element_type=F32) + bias
            m = jnp.max(s, axis=-1, keepdims=True)
            p = jnp.exp2(s - m).astype(BF16)
            r = jnp.dot(p, vb[keys], preferred_element_type=F32)
            mb = jnp.broadcast_to(m, (HEADS_PER_STEP * blk, V7X_LANES))
            ms.append(jnp.where(first_head, mb[:blk], mb[blk:]))
            pvs.append(jnp.where(first_head, r[:blk, :V7X_LANES], r[blk:, :V7X_LANES]))
            ls.append(jnp.where(first_head, r[:blk, V7X_LANES:], r[blk:, V7X_LANES:]))
        m_new = jnp.concatenate(ms, axis=0)
        l_new = jnp.concatenate(ls, axis=0)
        acc = jnp.concatenate(pvs, axis=0)

        if not first_branch:
            m_old, l_old, acc_old = load(m_cls, 0, grp), load(l_cls, 0, grp), load(o_cls, 0, grp)
            top = jnp.maximum(m_old, m_new)
            w_old = jnp.exp2(m_old - top)
            w_new = jnp.exp2(m_new - top)
            l_new = w_old * l_old + w_new * l_new
            acc = w_old * acc_old + w_new * acc
            m_new = top
        if last_branch:
            store(o_cls, acc / l_new)
        else:
            store(m_cls, m_new)
            store(l_cls, l_new)
            store(o_cls, acc)

    def bias_tables(dil, natural):
        def position(idx):
            if not natural:
                return idx
            inside = idx & (blk - 1)
            return (idx - inside) + ncls * (inside & (sub - 1)) + (inside >> int(math.log2(sub)))
        iq = position(lax.broadcasted_iota(jnp.int32, (blk, 2 * blk), 0))
        ik = position(lax.broadcasted_iota(jnp.int32, (blk, 2 * blk), 1))
        rel = blk + iq - ik
        valid = (rel >= 0) & (rel <= ATTN_BLOCK)
        dist = (dil * rel).astype(F32)
        two = jnp.concatenate(
            [jnp.where(valid, (-LOG2_E * slope[h]) * dist, NEG_INF) for h in range(HEADS_PER_STEP)],
            axis=0)
        return two[:, blk:], two

    rows_per_class = seq // ncls
    for branch, (window, dil) in enumerate(DILATED_BRANCHES):
        assert window // dil == ATTN_BLOCK
        first_branch, last_branch = branch == 0, branch == len(DILATED_BRANCHES) - 1
        bias_one, bias_two = bias_tables(dil, natural=(dil == 1))
        run = functools.partial(run_group, bias_one=bias_one, bias_two=bias_two,
                                first_branch=first_branch, last_branch=last_branch)

        def sweep(make_access, n_groups, run=run):
            run(make_access(0), True)
            if n_groups > 1:
                def later(j, carry):
                    run(make_access(j), False)
                    return carry
                lax.fori_loop(1, n_groups, later, 0, unroll=ATTN_UNROLL[dil])

        if dil == 1:
            sweep(natural_rows, seq // (grp * blk))
        elif dil == ncls:
            for c in range(ncls):
                sweep(functools.partial(class_rows, c), rows_per_class // (grp * blk))
        else:
            step = dil // ncls
            assert rows_per_class // step == grp * blk
            for c in range(ncls):
                def subclass(a, carry, c=c, step=step, run=run):
                    run(subclass_rows(c, a, step), True)
                    return carry
                lax.fori_loop(0, step, subclass, 0, unroll=ATTN_UNROLL[dil])


def _attention(slopes, qkv):
    ncls, bsz, rows, _ = qkv.shape
    n_pairs = N_HEADS // HEADS_PER_STEP
    block_bytes = ncls * rows * V7X_LANES * 4
    est = (2 * 4 * block_bytes
           + 2 * block_bytes
           + 6 * 1024 * 1024)

    def cols(offset):
        return pl.BlockSpec((ncls, 1, rows, V7X_LANES), lambda b, p: (0, b, 0, offset + p))

    return pl.pallas_call(
        functools.partial(_attention_kernel, seq=ncls * rows),
        out_shape=jax.ShapeDtypeStruct((ncls, bsz, rows, ATTN_WIDTH), F32),
        grid=(bsz, n_pairs),
        in_specs=[
            pl.BlockSpec(memory_space=pltpu.SMEM),
            cols(0), cols(n_pairs), cols(2 * n_pairs),
        ],
        out_specs=cols(0),
        scratch_shapes=[pltpu.VMEM((ncls, rows, V7X_LANES), F32),
                        pltpu.VMEM((ncls, rows, V7X_LANES), F32)],
        compiler_params=pltpu.CompilerParams(
            dimension_semantics=("arbitrary", "arbitrary"),
            vmem_limit_bytes=_vmem_limit(est)),
        name="dilated_attention",
    )(slopes, qkv, qkv, qkv)


def _alibi_slopes():
    return np.array([2.0 ** (-8.0 * (i + 1) / N_HEADS) for i in range(N_HEADS)], dtype=np.float32)


def kernel(x, ffn1_norm, ffn1_w_gate_up, ffn1_w_down, mix_norm, ffn2_norm, ffn2_w_gate_up,
           ffn2_w_down, conv_w_in, conv_a_kernel, conv_b_kernel, conv_b_bias, conv_b_ln_gain,
           conv_b_ln_bias, conv_w_out, attn_w_qkv, attn_w_o, final_norm):
    bsz, seq, d = x.shape
    n_tok = bsz * seq
    rows = lambda v: v.reshape(v.shape[0], 1, v.shape[1])
    bf16 = lambda w: w.astype(BF16)
    slopes = jnp.asarray(_alibi_slopes())
    g_final = final_norm.reshape(1, -1)
    ffn1_norm, mix_norm, ffn2_norm = rows(ffn1_norm), rows(mix_norm), rows(ffn2_norm)
    conv_b_bias, conv_b_ln_gain, conv_b_ln_bias = (
        rows(conv_b_bias), rows(conv_b_ln_gain), rows(conv_b_ln_bias))
    ffn1_wgu, ffn1_wd = bf16(ffn1_w_gate_up), bf16(ffn1_w_down)
    ffn2_wgu, ffn2_wd = bf16(ffn2_w_gate_up), bf16(ffn2_w_down)
    conv_w_in, conv_w_out = bf16(conv_w_in), bf16(conv_w_out)
    attn_w_qkv, attn_w_o = bf16(attn_w_qkv), bf16(attn_w_o)

    for layer in range(DEPTH):
        x = _ffn(x.reshape(n_tok, d), ffn1_norm, ffn1_wgu, ffn1_wd, g_final,
                 layer=layer, final_norm=False).reshape(bsz, seq, d)
        i = layer // 2
        if layer % 2 == 0:
            x = _conv_mixer(x, mix_norm, conv_w_in, conv_a_kernel, conv_b_kernel, conv_b_bias,
                            conv_b_ln_gain, conv_b_ln_bias, conv_w_out, layer=layer, w_layer=i)
        else:
            qkv = _norm_proj(x, mix_norm, attn_w_qkv, layer=layer, w_layer=i)
            o = _attention(slopes, qkv)
            x = _proj_residual(x, o, attn_w_o, w_layer=i)
        x = _ffn(x.reshape(n_tok, d), ffn2_norm, ffn2_wgu, ffn2_wd, g_final,
                 layer=layer, final_norm=(layer == DEPTH - 1)).reshape(bsz, seq, d)
    return x
```

```python
import functools
import math

import numpy as np
import jax
import jax.numpy as jnp
from jax import lax
from jax.experimental import pallas as pl
from jax.experimental.pallas import tpu as pltpu

D_MODEL = 1024
DEPTH = 4
D_FF = 2816
RMS_EPS = 1e-6
LN_EPS = 1e-5
SC_WIDTH = 512
SC_KERNEL = 3
CM_WIDTH = 512
CM_KERNEL = 31
IN_AB = 3 * SC_WIDTH + 2 * CM_WIDTH
N_HEADS = 16
HEAD_DIM = 64
ATTN_WIDTH = N_HEADS * HEAD_DIM
DILATED_BRANCHES = ((128, 1), (512, 4), (2048, 16))
ATTN_BLOCK = 128
NEG_INF = -1e30

V7X_LANES = 128
V7X_SUBLANES = 8
V7X_MXU_COLS = 256
V7X_VMEM_BYTES = 64 * 1024 * 1024

F32 = jnp.float32
BF16 = jnp.bfloat16

TOKEN_TILE = 512
FFN_TILE = 1024
FF_CHUNK = V7X_MXU_COLS
CONV_ROWS = 64
CONV_A_HALO = V7X_SUBLANES
CONV_B_HALO = 32
HEADS_PER_STEP = V7X_LANES // HEAD_DIM
ATTN_GROUP = 4
ATTN_CLASSES = 4
ATTN_UNROLL = {1: 5, 4: 3, 16: 2}
LOG2_E = math.log2(math.e)


def _vmem_limit(estimate_bytes):
    return int(min(V7X_VMEM_BYTES - 8 * 1024 * 1024, estimate_bytes))


def _rms_norm(x, g):
    return x * lax.rsqrt(jnp.mean(x * x, axis=-1, keepdims=True) + RMS_EPS) * g


def _sigmoid(x):
    return 1.0 / (1.0 + jnp.exp(-x))


def _resident(shape, layer=None):
    zeros = (0,) * len(shape)
    if layer is None:
        return pl.BlockSpec(shape, lambda *_: zeros, pipeline_mode=pl.Buffered(1))
    return pl.BlockSpec((None,) + tuple(shape), lambda *_: (layer,) + zeros,
                        pipeline_mode=pl.Buffered(1))


def _ffn_tile(x, g_ref, wgu_ref, wd_ref, gf_ref, a_scr, final_norm):
    h = _rms_norm(x, g_ref[...]).astype(BF16)
    for c in range(D_FF // FF_CHUNK):
        lo = c * FF_CHUNK
        gate = jnp.dot(h, wgu_ref[:, lo:lo + FF_CHUNK], preferred_element_type=F32)
        up = jnp.dot(h, wgu_ref[:, D_FF + lo:D_FF + lo + FF_CHUNK], preferred_element_type=F32)
        a_scr[:, lo:lo + FF_CHUNK] = (gate * _sigmoid(gate) * up).astype(BF16)
    y = x + 0.5 * jnp.dot(a_scr[...], wd_ref[...], preferred_element_type=F32)
    if final_norm:
        y = _rms_norm(y, gf_ref[...])
    return y


def _ffn_kernel(x_ref, g_ref, wgu_ref, wd_ref, gf_ref, o_ref, a_scr, *, final_norm):
    o_ref[...] = _ffn_tile(x_ref[...], g_ref, wgu_ref, wd_ref, gf_ref, a_scr, final_norm)


def _ffn_vmem_estimate(tm):
    return (4 * tm * D_MODEL * 4
            + 3 * D_MODEL * D_FF * 2
            + tm * D_FF * 2
            + 6 * tm * D_MODEL * 4)


def _ffn(x2d, g, wgu, wd, g_final, *, layer, final_norm):
    n_tok = x2d.shape[0]
    tm = FFN_TILE
    est = _ffn_vmem_estimate(tm)
    return pl.pallas_call(
        functools.partial(_ffn_kernel, final_norm=final_norm),
        out_shape=jax.ShapeDtypeStruct((n_tok, D_MODEL), F32),
        grid=(n_tok // tm,),
        in_specs=[
            pl.BlockSpec((tm, D_MODEL), lambda i: (i, 0)),
            _resident((1, D_MODEL), layer),
            _resident((D_MODEL, 2 * D_FF), layer),
            _resident((D_FF, D_MODEL), layer),
            _resident((1, D_MODEL)),
        ],
        out_specs=pl.BlockSpec((tm, D_MODEL), lambda i: (i, 0)),
        scratch_shapes=[pltpu.VMEM((tm, D_FF), BF16)],
        compiler_params=pltpu.CompilerParams(
            dimension_semantics=("arbitrary",), vmem_limit_bytes=_vmem_limit(est)),
        name="ffn_final" if final_norm else "ffn",
    )(x2d, g, wgu, wd, g_final)


def _class_rows(c, tm):
    return pl.ds(c, tm // ATTN_CLASSES, stride=ATTN_CLASSES)


def _lane_block(j):
    return slice(j * V7X_LANES, (j + 1) * V7X_LANES)


def _to_class_major(tile, scr, tm):
    n_blocks = tile.shape[1] // V7X_LANES
    for j in range(n_blocks):
        scr[j] = tile[:, _lane_block(j)]
    return jnp.concatenate(
        [jnp.concatenate([scr[j, _class_rows(c, tm), :] for j in range(n_blocks)], axis=1)
         for c in range(ATTN_CLASSES)], axis=0)


def _from_class_major(tile, scr, tm):
    n_blocks = tile.shape[1] // V7X_LANES
    per_class = tm // ATTN_CLASSES
    for j in range(n_blocks):
        for c in range(ATTN_CLASSES):
            scr[j, _class_rows(c, tm), :] = tile[c * per_class:(c + 1) * per_class, _lane_block(j)]
    return jnp.concatenate([scr[j] for j in range(n_blocks)], axis=1)


def _norm_proj_kernel(x_ref, g_ref, w_ref, o_ref, scr, *, tm):
    per_class = tm // ATTN_CLASSES
    h = _rms_norm(_to_class_major(x_ref[0], scr, tm), g_ref[...]).astype(BF16)
    y = jnp.dot(h, w_ref[...], preferred_element_type=F32)
    for c in range(ATTN_CLASSES):
        o_ref[c, 0] = y[c * per_class:(c + 1) * per_class]


def _norm_proj(x, g, w, *, layer, w_layer):
    bsz, seq, _ = x.shape
    n_out = w.shape[-1]
    tm = TOKEN_TILE
    per_class = tm // ATTN_CLASSES
    est = (2 * tm * D_MODEL * 4 + 2 * tm * n_out * 4 + D_MODEL * n_out * 2 + 2 * tm * n_out * 4)
    return pl.pallas_call(
        functools.partial(_norm_proj_kernel, tm=tm),
        out_shape=jax.ShapeDtypeStruct((ATTN_CLASSES, bsz, seq // ATTN_CLASSES, n_out), F32),
        grid=(bsz, seq // tm),
        in_specs=[
            pl.BlockSpec((1, tm, D_MODEL), lambda b, i: (b, i, 0)),
            _resident((1, D_MODEL), layer),
            _resident(w.shape[1:], w_layer),
        ],
        out_specs=pl.BlockSpec((ATTN_CLASSES, 1, per_class, n_out), lambda b, i: (0, b, i, 0)),
        scratch_shapes=[pltpu.VMEM((D_MODEL // V7X_LANES, tm, V7X_LANES), F32)],
        compiler_params=pltpu.CompilerParams(
            dimension_semantics=("arbitrary",) * 2, vmem_limit_bytes=_vmem_limit(est)),
        name="norm_proj",
    )(x, g, w)


def _attn_out_ffn_kernel(x_ref, y_ref, wo_ref, g_ref, wgu_ref, wd_ref, gf_ref, o_ref,
                         a_scr, cm_scr, *, tm, final_norm):
    y = jnp.concatenate([y_ref[c, 0] for c in range(ATTN_CLASSES)], axis=0).astype(BF16)
    r = jnp.dot(y, wo_ref[...], preferred_element_type=F32)
    x = x_ref[0] + _from_class_major(r, cm_scr, tm)
    o_ref[0] = _ffn_tile(x, g_ref, wgu_ref, wd_ref, gf_ref, a_scr, final_norm)


def _attn_out_ffn(x, y, w_o, g, wgu, wd, g_final, *, layer, w_layer, final_norm):
    bsz, seq, _ = x.shape
    tm = TOKEN_TILE
    per_class = tm // ATTN_CLASSES
    est = (_ffn_vmem_estimate(tm) + ATTN_WIDTH * D_MODEL * 2
           + 2 * tm * ATTN_WIDTH * 4
           + 3 * tm * D_MODEL * 4)
    x_tile = pl.BlockSpec((1, tm, D_MODEL), lambda b, i: (b, i, 0))
    return pl.pallas_call(
        functools.partial(_attn_out_ffn_kernel, tm=tm, final_norm=final_norm),
        out_shape=jax.ShapeDtypeStruct(x.shape, F32),
        grid=(bsz, seq // tm),
        in_specs=[
            x_tile,
            pl.BlockSpec((ATTN_CLASSES, 1, per_class, y.shape[-1]), lambda b, i: (0, b, i, 0)),
            _resident(w_o.shape[1:], w_layer),
            _resident((1, D_MODEL), layer),
            _resident((D_MODEL, 2 * D_FF), layer),
            _resident((D_FF, D_MODEL), layer),
            _resident((1, D_MODEL)),
        ],
        out_specs=x_tile,
        scratch_shapes=[pltpu.VMEM((tm, D_FF), BF16),
                        pltpu.VMEM((D_MODEL // V7X_LANES, tm, V7X_LANES), F32)],
        compiler_params=pltpu.CompilerParams(
            dimension_semantics=("arbitrary",) * 2, vmem_limit_bytes=_vmem_limit(est)),
        name="attn_out_ffn_final" if final_norm else "attn_out_ffn",
    )(x, y, w_o, g, wgu, wd, g_final)


def _conv_mixer_kernel(x_ref, g_ref, w_in_ref, ka_ref, kb_ref, bb_ref, lng_ref, lnb_ref,
                       w_out_ref, o_ref, cx_scr, u_scr, cv_scr, *, tm):
    ha, hb = CONV_A_HALO, CONV_B_HALO

    @pl.when(pl.program_id(1) == 0)
    def _start_of_sequence():
        cx_scr[0:ha, :] = jnp.zeros((ha, SC_WIDTH), F32)
        u_scr[0:hb, :] = jnp.zeros((hb, CM_WIDTH), F32)

    x = x_ref[0]
    h = _rms_norm(x, g_ref[...]).astype(BF16)

    def proj(lo, width):
        return jnp.dot(h, w_in_ref[:, lo:lo + width], preferred_element_type=F32)

    u_scr[hb:hb + tm, :] = proj(3 * SC_WIDTH, CM_WIDTH) * _sigmoid(
        proj(3 * SC_WIDTH + CM_WIDTH, CM_WIDTH))

    first_off = hb - (CM_KERNEL - 1)

    def conv_b_rows(r0):
        for j in range(CM_WIDTH // V7X_LANES):
            lanes = slice(j * V7X_LANES, (j + 1) * V7X_LANES)
            n_win = CONV_ROWS + hb
            window = u_scr[pl.ds(r0, n_win), lanes]
            acc = jnp.zeros((CONV_ROWS, V7X_LANES), F32)
            for sub in range(V7X_SUBLANES):
                offs = [o for o in range(first_off, hb + 1) if o % V7X_SUBLANES == sub]
                shifted = pltpu.roll(window, n_win - sub, axis=0) if sub else window
                for o in offs:
                    k = o - first_off
                    acc = acc + kb_ref[k:k + 1, lanes] * shifted[o - sub:o - sub + CONV_ROWS, :]
            cv_scr[pl.ds(r0, CONV_ROWS), lanes] = acc

    a_b = []

    def mixer_a_cols(lo):
        a_b.append(proj(lo, V7X_MXU_COLS))
        cx_scr[ha:ha + tm, lo:lo + V7X_MXU_COLS] = (
            proj(SC_WIDTH + lo, V7X_MXU_COLS) * proj(2 * SC_WIDTH + lo, V7X_MXU_COLS))

    def mixer_a_output():
        conv_a = ka_ref[0:1, :] * cx_scr[ha - 2:ha - 2 + tm, :]
        conv_a = conv_a + ka_ref[1:2, :] * cx_scr[ha - 1:ha - 1 + tm, :]
        conv_a = conv_a + ka_ref[2:3, :] * cx_scr[ha:ha + tm, :]
        y_a = (jnp.concatenate(a_b, axis=1) * conv_a).astype(BF16)
        return jnp.dot(y_a, w_out_ref[0:SC_WIDTH, :], preferred_element_type=F32)

    out_a = None
    for idx, r0 in enumerate(range(0, tm, CONV_ROWS)):
        conv_b_rows(r0)
        if idx < SC_WIDTH // V7X_MXU_COLS:
            mixer_a_cols(idx * V7X_MXU_COLS)
        elif out_a is None:
            out_a = mixer_a_output()

    u = cv_scr[...] + bb_ref[...]
    mu = jnp.mean(u, axis=-1, keepdims=True)
    uc = u - mu
    un = uc * lax.rsqrt(jnp.mean(uc * uc, axis=-1, keepdims=True) + LN_EPS)
    un = un * lng_ref[...] + lnb_ref[...]
    y_b = (un * _sigmoid(un)).astype(BF16)
    o_ref[0] = x + out_a + jnp.dot(y_b, w_out_ref[SC_WIDTH:SC_WIDTH + CM_WIDTH, :],
                                   preferred_element_type=F32)

    cx_scr[0:ha, :] = cx_scr[tm:tm + ha, :]
    u_scr[0:hb, :] = u_scr[tm:tm + hb, :]


def _conv_mixer(x, g, w_in, ka, kb, bb, lng, lnb, w_out, *, layer, w_layer):
    bsz, seq, _ = x.shape
    tm = TOKEN_TILE
    est = (4 * tm * D_MODEL * 4 + (D_MODEL * IN_AB + D_MODEL * D_MODEL) * 2
           + (3 * tm + 64) * SC_WIDTH * 4 + tm * D_MODEL * 2
           + 8 * tm * D_MODEL * 4)
    tile = pl.BlockSpec((1, tm, D_MODEL), lambda b, t: (b, t, 0))
    return pl.pallas_call(
        functools.partial(_conv_mixer_kernel, tm=tm),
        out_shape=jax.ShapeDtypeStruct(x.shape, F32),
        grid=(bsz, seq // tm),
        in_specs=[
            tile,
            _resident((1, D_MODEL), layer),
            _resident(w_in.shape[1:], w_layer),
            _resident(ka.shape[1:], w_layer),
            _resident(kb.shape[1:], w_layer),
            _resident((1, CM_WIDTH), w_layer),
            _resident((1, CM_WIDTH), w_layer),
            _resident((1, CM_WIDTH), w_layer),
            _resident(w_out.shape[1:], w_layer),
        ],
        out_specs=tile,
        scratch_shapes=[
            pltpu.VMEM((CONV_A_HALO + tm, SC_WIDTH), F32),
            pltpu.VMEM((CONV_B_HALO + tm, CM_WIDTH), F32),
            pltpu.VMEM((tm, CM_WIDTH), F32),
        ],
        compiler_params=pltpu.CompilerParams(
            dimension_semantics=("arbitrary", "arbitrary"),
            vmem_limit_bytes=_vmem_limit(est)),
        name="conv_mixer",
    )(x, g, w_in, ka, kb, bb, lng, lnb, w_out)


def _attention_kernel(slopes_ref, q_ref, k_ref, v_ref, o_ref, m_scr, l_scr, *, seq):
    blk, grp, ncls = ATTN_BLOCK, ATTN_GROUP, ATTN_CLASSES
    sub = blk // ncls
    pair = pl.program_id(1)
    slope = [slopes_ref[HEADS_PER_STEP * pair + h] for h in range(HEADS_PER_STEP)]
    first_head = lax.broadcasted_iota(jnp.int32, (1, V7X_LANES), 1) < HEAD_DIM
    head_lanes = [first_head, jnp.logical_not(first_head)]
    nt_dims = (((1,), (1,)), ((), ()))
    q_scale = LOG2_E / math.sqrt(HEAD_DIM)

    q_cls = [q_ref.at[c, 0] for c in range(ncls)]
    k_cls = [k_ref.at[c, 0] for c in range(ncls)]
    v_cls = [v_ref.at[c, 0] for c in range(ncls)]
    o_cls = [o_ref.at[c, 0] for c in range(ncls)]
    m_cls = [m_scr.at[c] for c in range(ncls)]
    l_cls = [l_scr.at[c] for c in range(ncls)]

    def class_rows(c, j):
        def load(refs, first, n):
            start = pl.multiple_of((j * grp + first) * blk, blk)
            return refs[c][pl.ds(start, n * blk), :]

        def store(refs, value):
            refs[c][pl.ds(pl.multiple_of(j * grp * blk, blk), grp * blk), :] = value
        return load, store

    def subclass_rows(c, a, step):
        def load(refs, first, n):
            assert first == 0 and n == grp
            return refs[c][pl.ds(a, grp * blk, stride=step), :]

        def store(refs, value):
            refs[c][pl.ds(a, grp * blk, stride=step), :] = value
        return load, store

    def natural_rows(j):
        def load(refs, first, n):
            start = pl.multiple_of((j * grp + first) * sub, sub)
            per_class = [refs[c][pl.ds(start, n * sub), :] for c in range(ncls)]
            return jnp.concatenate(
                [per_class[c][b * sub:(b + 1) * sub] for b in range(n) for c in range(ncls)], axis=0)

        def store(refs, value):
            start = pl.multiple_of(j * grp * sub, sub)
            for c in range(ncls):
                refs[c][pl.ds(start, grp * sub), :] = jnp.concatenate(
                    [value[b * blk + c * sub:b * blk + (c + 1) * sub] for b in range(grp)], axis=0)
        return load, store

    def run_group(access, first_group, bias_one, bias_two, first_branch, last_branch):
        load, store = access
        q = load(q_cls, 0, grp) * q_scale
        k_first, k_blocks = (0, grp) if first_group else (-1, grp + 1)
        kb = load(k_cls, k_first, k_blocks).astype(BF16)
        vb = load(v_cls, k_first, k_blocks).astype(BF16)
        vb = jnp.concatenate([vb, jnp.ones(vb.shape, BF16)], axis=1)
        qh = [jnp.where(head_lanes[h], q, 0.0).astype(BF16) for h in range(HEADS_PER_STEP)]

        ms, ls, pvs = [], [], []
        for u in range(grp):
            if first_group and u == 0:
                keys, bias = slice(0, blk), bias_one
            else:
                lo = u - 1 if first_group else u
                keys, bias = slice(lo * blk, (lo + 2) * blk), bias_two
            rows = slice(u * blk, (u + 1) * blk)
            lhs = jnp.concatenate([qh[h][rows] for h in range(HEADS_PER_STEP)], axis=0)
            s = lax.dot_general(lhs, kb[keys], nt_dims, preferred_element_type=F32) + bias
            m = jnp.max(s, axis=-1, keepdims=True)
            p = jnp.exp2(s - m).astype(BF16)
            r = jnp.dot(p, vb[keys], preferred_element_type=F32)
            mb = jnp.broadcast_to(m, (HEADS_PER_STEP * blk, V7X_LANES))
            ms.append(jnp.where(first_head, mb[:blk], mb[blk:]))
            pvs.append(jnp.where(first_head, r[:blk, :V7X_LANES], r[blk:, :V7X_LANES]))
            ls.append(jnp.where(first_head, r[:blk, V7X_LANES:], r[blk:, V7X_LANES:]))
        m_new = jnp.concatenate(ms, axis=0)
        l_new = jnp.concatenate(ls, axis=0)
        acc = jnp.concatenate(pvs, axis=0)

        if not first_branch:
            m_old, l_old, acc_old = load(m_cls, 0, grp), load(l_cls, 0, grp), load(o_cls, 0, grp)
            top = jnp.maximum(m_old, m_new)
            w_old = jnp.exp2(m_old - top)
            w_new = jnp.exp2(m_new - top)
            l_new = w_old * l_old + w_new * l_new
            acc = w_old * acc_old + w_new * acc
            m_new = top
        if last_branch:
            store(o_cls, acc / l_new)
        else:
            store(m_cls, m_new)
            store(l_cls, l_new)
            store(o_cls, acc)

    def bias_tables(dil, natural):
        def position(idx):
            if not natural:
                return idx
            inside = idx & (blk - 1)
            return (idx - inside) + ncls * (inside & (sub - 1)) + (inside >> int(math.log2(sub)))
        iq = position(lax.broadcasted_iota(jnp.int32, (blk, 2 * blk), 0))
        ik = position(lax.broadcasted_iota(jnp.int32, (blk, 2 * blk), 1))
        rel = blk + iq - ik
        valid = (rel >= 0) & (rel <= ATTN_BLOCK)
        dist = (dil * rel).astype(F32)
        two = jnp.concatenate(
            [jnp.where(valid, (-LOG2_E * slope[h]) * dist, NEG_INF) for h in range(HEADS_PER_STEP)],
            axis=0)
        return two[:, blk:], two

    rows_per_class = seq // ncls
    for branch, (window, dil) in enumerate(DILATED_BRANCHES):
        assert window // dil == ATTN_BLOCK
        first_branch, last_branch = branch == 0, branch == len(DILATED_BRANCHES) - 1
        bias_one, bias_two = bias_tables(dil, natural=(dil == 1))
        run = functools.partial(run_group, bias_one=bias_one, bias_two=bias_two,
                                first_branch=first_branch, last_branch=last_branch)

        def sweep(make_access, n_groups, run=run):
            run(make_access(0), True)
            if n_groups > 1:
                def later(j, carry):
                    run(make_access(j), False)
                    return carry
                lax.fori_loop(1, n_groups, later, 0, unroll=ATTN_UNROLL[dil])

        if dil == 1:
            sweep(natural_rows, seq // (grp * blk))
        elif dil == ncls:
            for c in range(ncls):
                sweep(functools.partial(class_rows, c), rows_per_class // (grp * blk))
        else:
            step = dil // ncls
            assert rows_per_class // step == grp * blk
            for c in range(ncls):
                def subclass(a, carry, c=c, step=step, run=run):
                    run(subclass_rows(c, a, step), True)
                    return carry
                lax.fori_loop(0, step, subclass, 0, unroll=ATTN_UNROLL[dil])


def _attention(slopes, qkv):
    ncls, bsz, rows, _ = qkv.shape
    n_pairs = N_HEADS // HEADS_PER_STEP
    block_bytes = ncls * rows * V7X_LANES * 4
    est = (2 * 4 * block_bytes
           + 2 * block_bytes
           + 6 * 1024 * 1024)

    def cols(offset):
        return pl.BlockSpec((ncls, 1, rows, V7X_LANES), lambda b, p: (0, b, 0, offset + p))

    return pl.pallas_call(
        functools.partial(_attention_kernel, seq=ncls * rows),
        out_shape=jax.ShapeDtypeStruct((ncls, bsz, rows, ATTN_WIDTH), F32),
        grid=(bsz, n_pairs),
        in_specs=[
            pl.BlockSpec(memory_space=pltpu.SMEM),
            cols(0), cols(n_pairs), cols(2 * n_pairs),
        ],
        out_specs=cols(0),
        scratch_shapes=[pltpu.VMEM((ncls, rows, V7X_LANES), F32),
                        pltpu.VMEM((ncls, rows, V7X_LANES), F32)],
        compiler_params=pltpu.CompilerParams(
            dimension_semantics=("arbitrary", "arbitrary"),
            vmem_limit_bytes=_vmem_limit(est)),
        name="dilated_attention",
    )(slopes, qkv, qkv, qkv)


def _alibi_slopes():
    return np.array([2.0 ** (-8.0 * (i + 1) / N_HEADS) for i in range(N_HEADS)], dtype=np.float32)


def kernel(x, ffn1_norm, ffn1_w_gate_up, ffn1_w_down, mix_norm, ffn2_norm, ffn2_w_gate_up,
           ffn2_w_down, conv_w_in, conv_a_kernel, conv_b_kernel, conv_b_bias, conv_b_ln_gain,
           conv_b_ln_bias, conv_w_out, attn_w_qkv, attn_w_o, final_norm):
    bsz, seq, d = x.shape
    n_tok = bsz * seq
    rows = lambda v: v.reshape(v.shape[0], 1, v.shape[1])
    bf16 = lambda w: w.astype(BF16)
    slopes = jnp.asarray(_alibi_slopes())
    g_final = final_norm.reshape(1, -1)
    ffn1_norm, mix_norm, ffn2_norm = rows(ffn1_norm), rows(mix_norm), rows(ffn2_norm)
    conv_b_bias, conv_b_ln_gain, conv_b_ln_bias = (
        rows(conv_b_bias), rows(conv_b_ln_gain), rows(conv_b_ln_bias))
    ffn1_wgu, ffn1_wd = bf16(ffn1_w_gate_up), bf16(ffn1_w_down)
    ffn2_wgu, ffn2_wd = bf16(ffn2_w_gate_up), bf16(ffn2_w_down)
    conv_w_in, conv_w_out = bf16(conv_w_in), bf16(conv_w_out)
    attn_w_qkv, attn_w_o = bf16(attn_w_qkv), bf16(attn_w_o)

    for layer in range(DEPTH):
        x = _ffn(x.reshape(n_tok, d), ffn1_norm, ffn1_wgu, ffn1_wd, g_final,
                 layer=layer, final_norm=False).reshape(bsz, seq, d)
        i = layer // 2
        last = layer == DEPTH - 1
        if layer % 2 == 0:
            x = _conv_mixer(x, mix_norm, conv_w_in, conv_a_kernel, conv_b_kernel, conv_b_bias,
                            conv_b_ln_gain, conv_b_ln_bias, conv_w_out, layer=layer, w_layer=i)
            x = _ffn(x.reshape(n_tok, d), ffn2_norm, ffn2_wgu, ffn2_wd, g_final,
                     layer=layer, final_norm=last).reshape(bsz, seq, d)
        else:
            qkv = _norm_proj(x, mix_norm, attn_w_qkv, layer=layer, w_layer=i)
            o = _attention(slopes, qkv)
            x = _attn_out_ffn(x, o, attn_w_o, ffn2_norm, ffn2_wgu, ffn2_wd, g_final,
                              layer=layer, w_layer=i, final_norm=last)
    return x
```

```python
import functools
import math

import numpy as np
import jax
import jax.numpy as jnp
from jax import lax
from jax.experimental import pallas as pl
from jax.experimental.pallas import tpu as pltpu

D_MODEL = 1024
DEPTH = 4
D_FF = 2816
RMS_EPS = 1e-6
LN_EPS = 1e-5
SC_WIDTH = 512
SC_KERNEL = 3
CM_WIDTH = 512
CM_KERNEL = 31
IN_AB = 3 * SC_WIDTH + 2 * CM_WIDTH
N_HEADS = 16
HEAD_DIM = 64
ATTN_WIDTH = N_HEADS * HEAD_DIM
DILATED_BRANCHES = ((128, 1), (512, 4), (2048, 16))
ATTN_BLOCK = 128
NEG_INF = -1e30

V7X_LANES = 128
V7X_SUBLANES = 8
V7X_MXU_COLS = 256
V7X_VMEM_BYTES = 64 * 1024 * 1024

F32 = jnp.float32
BF16 = jnp.bfloat16

TOKEN_TILE = 512
FFN_TILE = 1024
FF_CHUNK = V7X_MXU_COLS
WGU_CHUNK_ROWS = 128
WD_CHUNK_ROWS = 256
CONV_ROWS = 64
CONV_A_HALO = V7X_SUBLANES
CONV_B_HALO = 32
HEADS_PER_STEP = V7X_LANES // HEAD_DIM
ATTN_GROUP = 4
ATTN_CLASSES = 4
ATTN_UNROLL = {1: 5, 4: 3, 16: 2}
LOG2_E = math.log2(math.e)


def _vmem_limit(estimate_bytes):
    return int(min(V7X_VMEM_BYTES - 8 * 1024 * 1024, estimate_bytes))


def _rms_norm(x, g):
    return x * lax.rsqrt(jnp.mean(x * x, axis=-1, keepdims=True) + RMS_EPS) * g


def _sigmoid(x):
    return 1.0 / (1.0 + jnp.exp(-x))


def _resident(shape, layer=None):
    zeros = (0,) * len(shape)
    if layer is None:
        return pl.BlockSpec(shape, lambda *_: zeros, pipeline_mode=pl.Buffered(1))
    return pl.BlockSpec((None,) + tuple(shape), lambda *_: (layer,) + zeros,
                        pipeline_mode=pl.Buffered(1))


def _ffn_tile(x, g_ref, wgu_ref, wd_ref, gf_ref, a_scr, final_norm):
    h = _rms_norm(x, g_ref[...]).astype(BF16)
    for c in range(D_FF // FF_CHUNK):
        lo = c * FF_CHUNK
        gate = jnp.dot(h, wgu_ref[:, lo:lo + FF_CHUNK], preferred_element_type=F32)
        up = jnp.dot(h, wgu_ref[:, D_FF + lo:D_FF + lo + FF_CHUNK], preferred_element_type=F32)
        a_scr[:, lo:lo + FF_CHUNK] = (gate * _sigmoid(gate) * up).astype(BF16)
    y = x + 0.5 * jnp.dot(a_scr[...], wd_ref[...], preferred_element_type=F32)
    if final_norm:
        y = _rms_norm(y, gf_ref[...])
    return y


def _load_weight_bf16(src_hbm, dst, stage, sems, chunk_rows):
    n_chunks = src_hbm.shape[0] // chunk_rows

    def chunk_copy(c):
        slot = c % 2
        return pltpu.make_async_copy(
            src_hbm.at[pl.ds(c * chunk_rows, chunk_rows), :], stage.at[slot], sems.at[slot])

    chunk_copy(0).start()
    for c in range(n_chunks):
        if c + 1 < n_chunks:
            chunk_copy(c + 1).start()
        chunk_copy(c).wait()
        dst[c * chunk_rows:(c + 1) * chunk_rows, :] = stage[c % 2].astype(BF16)


def _ffn_weights_resident(wgu_hbm, wd_hbm, scratch, first_step):
    wgu_bf, wd_bf, stage_gu, stage_d, sems = scratch

    @pl.when(first_step)
    def _load():
        _load_weight_bf16(wgu_hbm, wgu_bf, stage_gu, sems.at[0], WGU_CHUNK_ROWS)
        _load_weight_bf16(wd_hbm, wd_bf, stage_d, sems.at[1], WD_CHUNK_ROWS)
    return wgu_bf, wd_bf


def _ffn_weight_scratch():
    return [pltpu.VMEM((D_MODEL, 2 * D_FF), BF16),
            pltpu.VMEM((D_FF, D_MODEL), BF16),
            pltpu.VMEM((2, WGU_CHUNK_ROWS, 2 * D_FF), F32),
            pltpu.VMEM((2, WD_CHUNK_ROWS, D_MODEL), F32),
            pltpu.SemaphoreType.DMA((2, 2))]


def _ffn_kernel(x_ref, g_ref, wgu_hbm, wd_hbm, gf_ref, o_ref, a_scr, *weight_scratch,
                layer, final_norm):
    wgu_bf, wd_bf = _ffn_weights_resident(wgu_hbm.at[layer], wd_hbm.at[layer], weight_scratch,
                                          pl.program_id(0) == 0)
    o_ref[...] = _ffn_tile(x_ref[...], g_ref, wgu_bf, wd_bf, gf_ref, a_scr, final_norm)


def _ffn_vmem_estimate(tm):
    return (4 * tm * D_MODEL * 4
            + 3 * D_MODEL * D_FF * 2
            + 2 * (WGU_CHUNK_ROWS * 2 * D_FF + WD_CHUNK_ROWS * D_MODEL) * 4
            + tm * D_FF * 2
            + 6 * tm * D_MODEL * 4)


def _ffn(x2d, g, wgu, wd, g_final, *, layer, final_norm):
    n_tok = x2d.shape[0]
    tm = FFN_TILE
    est = _ffn_vmem_estimate(tm)
    return pl.pallas_call(
        functools.partial(_ffn_kernel, layer=layer, final_norm=final_norm),
        out_shape=jax.ShapeDtypeStruct((n_tok, D_MODEL), F32),
        grid=(n_tok // tm,),
        in_specs=[
            pl.BlockSpec((tm, D_MODEL), lambda i: (i, 0)),
            _resident((1, D_MODEL), layer),
            pl.BlockSpec(memory_space=pl.ANY),
            pl.BlockSpec(memory_space=pl.ANY),
            _resident((1, D_MODEL)),
        ],
        out_specs=pl.BlockSpec((tm, D_MODEL), lambda i: (i, 0)),
        scratch_shapes=[pltpu.VMEM((tm, D_FF), BF16)] + _ffn_weight_scratch(),
        compiler_params=pltpu.CompilerParams(
            dimension_semantics=("arbitrary",), vmem_limit_bytes=_vmem_limit(est)),
        name="ffn_final" if final_norm else "ffn",
    )(x2d, g, wgu, wd, g_final)


def _class_rows(c, tm):
    return pl.ds(c, tm // ATTN_CLASSES, stride=ATTN_CLASSES)


def _lane_block(j):
    return slice(j * V7X_LANES, (j + 1) * V7X_LANES)


def _to_class_major(tile, scr, tm):
    n_blocks = tile.shape[1] // V7X_LANES
    for j in range(n_blocks):
        scr[j] = tile[:, _lane_block(j)]
    return jnp.concatenate(
        [jnp.concatenate([scr[j, _class_rows(c, tm), :] for j in range(n_blocks)], axis=1)
         for c in range(ATTN_CLASSES)], axis=0)


def _from_class_major(tile, scr, tm):
    n_blocks = tile.shape[1] // V7X_LANES
    per_class = tm // ATTN_CLASSES
    for j in range(n_blocks):
        for c in range(ATTN_CLASSES):
            scr[j, _class_rows(c, tm), :] = tile[c * per_class:(c + 1) * per_class, _lane_block(j)]
    return jnp.concatenate([scr[j] for j in range(n_blocks)], axis=1)


def _norm_proj_kernel(x_ref, g_ref, w_ref, o_ref, scr, *, tm):
    per_class = tm // ATTN_CLASSES
    h = _rms_norm(_to_class_major(x_ref[0], scr, tm), g_ref[...]).astype(BF16)
    y = jnp.dot(h, w_ref[...], preferred_element_type=F32)
    for c in range(ATTN_CLASSES):
        o_ref[c, 0] = y[c * per_class:(c + 1) * per_class]


def _norm_proj(x, g, w, *, layer, w_layer):
    bsz, seq, _ = x.shape
    n_out = w.shape[-1]
    tm = TOKEN_TILE
    per_class = tm // ATTN_CLASSES
    est = (2 * tm * D_MODEL * 4 + 2 * tm * n_out * 4 + D_MODEL * n_out * 2 + 2 * tm * n_out * 4)
    return pl.pallas_call(
        functools.partial(_norm_proj_kernel, tm=tm),
        out_shape=jax.ShapeDtypeStruct((ATTN_CLASSES, bsz, seq // ATTN_CLASSES, n_out), F32),
        grid=(bsz, seq // tm),
        in_specs=[
            pl.BlockSpec((1, tm, D_MODEL), lambda b, i: (b, i, 0)),
            _resident((1, D_MODEL), layer),
            _resident(w.shape[1:], w_layer),
        ],
        out_specs=pl.BlockSpec((ATTN_CLASSES, 1, per_class, n_out), lambda b, i: (0, b, i, 0)),
        scratch_shapes=[pltpu.VMEM((D_MODEL // V7X_LANES, tm, V7X_LANES), F32)],
        compiler_params=pltpu.CompilerParams(
            dimension_semantics=("arbitrary",) * 2, vmem_limit_bytes=_vmem_limit(est)),
        name="norm_proj",
    )(x, g, w)


def _attn_out_ffn_kernel(x_ref, y_ref, wo_ref, g_ref, wgu_hbm, wd_hbm, gf_ref, o_ref,
                         a_scr, cm_scr, *weight_scratch, tm, layer, final_norm):
    first_step = (pl.program_id(0) == 0) & (pl.program_id(1) == 0)
    wgu_bf, wd_bf = _ffn_weights_resident(wgu_hbm.at[layer], wd_hbm.at[layer], weight_scratch,
                                          first_step)
    y = jnp.concatenate([y_ref[c, 0] for c in range(ATTN_CLASSES)], axis=0).astype(BF16)
    r = jnp.dot(y, wo_ref[...], preferred_element_type=F32)
    x = x_ref[0] + _from_class_major(r, cm_scr, tm)
    o_ref[0] = _ffn_tile(x, g_ref, wgu_bf, wd_bf, gf_ref, a_scr, final_norm)


def _attn_out_ffn(x, y, w_o, g, wgu, wd, g_final, *, layer, w_layer, final_norm):
    bsz, seq, _ = x.shape
    tm = TOKEN_TILE
    per_class = tm // ATTN_CLASSES
    est = (_ffn_vmem_estimate(tm) + ATTN_WIDTH * D_MODEL * 2
           + 2 * tm * ATTN_WIDTH * 4
           + 3 * tm * D_MODEL * 4)
    x_tile = pl.BlockSpec((1, tm, D_MODEL), lambda b, i: (b, i, 0))
    return pl.pallas_call(
        functools.partial(_attn_out_ffn_kernel, tm=tm, layer=layer, final_norm=final_norm),
        out_shape=jax.ShapeDtypeStruct(x.shape, F32),
        grid=(bsz, seq // tm),
        in_specs=[
            x_tile,
            pl.BlockSpec((ATTN_CLASSES, 1, per_class, y.shape[-1]), lambda b, i: (0, b, i, 0)),
            _resident(w_o.shape[1:], w_layer),
            _resident((1, D_MODEL), layer),
            pl.BlockSpec(memory_space=pl.ANY),
            pl.BlockSpec(memory_space=pl.ANY),
            _resident((1, D_MODEL)),
        ],
        out_specs=x_tile,
        scratch_shapes=[pltpu.VMEM((tm, D_FF), BF16),
                        pltpu.VMEM((D_MODEL // V7X_LANES, tm, V7X_LANES), F32)]
        + _ffn_weight_scratch(),
        compiler_params=pltpu.CompilerParams(
            dimension_semantics=("arbitrary",) * 2, vmem_limit_bytes=_vmem_limit(est)),
        name="attn_out_ffn_final" if final_norm else "attn_out_ffn",
    )(x, y, w_o, g, wgu, wd, g_final)


def _conv_mixer_kernel(x_ref, g_ref, w_in_ref, ka_ref, kb_ref, bb_ref, lng_ref, lnb_ref,
                       w_out_ref, o_ref, cx_scr, u_scr, cv_scr, *, tm):
    ha, hb = CONV_A_HALO, CONV_B_HALO

    @pl.when(pl.program_id(1) == 0)
    def _start_of_sequence():
        cx_scr[0:ha, :] = jnp.zeros((ha, SC_WIDTH), F32)
        u_scr[0:hb, :] = jnp.zeros((hb, CM_WIDTH), F32)

    x = x_ref[0]
    h = _rms_norm(x, g_ref[...]).astype(BF16)

    def proj(lo, width):
        return jnp.dot(h, w_in_ref[:, lo:lo + width], preferred_element_type=F32)

    u_scr[hb:hb + tm, :] = proj(3 * SC_WIDTH, CM_WIDTH) * _sigmoid(
        proj(3 * SC_WIDTH + CM_WIDTH, CM_WIDTH))

    first_off = hb - (CM_KERNEL - 1)

    def conv_b_rows(r0):
        for j in range(CM_WIDTH // V7X_LANES):
            lanes = slice(j * V7X_LANES, (j + 1) * V7X_LANES)
            n_win = CONV_ROWS + hb
            window = u_scr[pl.ds(r0, n_win), lanes]
            acc = jnp.zeros((CONV_ROWS, V7X_LANES), F32)
            for sub in range(V7X_SUBLANES):
                offs = [o for o in range(first_off, hb + 1) if o % V7X_SUBLANES == sub]
                shifted = pltpu.roll(window, n_win - sub, axis=0) if sub else window
                for o in offs:
                    k = o - first_off
                    acc = acc + kb_ref[k:k + 1, lanes] * shifted[o - sub:o - sub + CONV_ROWS, :]
            cv_scr[pl.ds(r0, CONV_ROWS), lanes] = acc

    a_b = []

    def mixer_a_cols(lo):
        a_b.append(proj(lo, V7X_MXU_COLS))
        cx_scr[ha:ha + tm, lo:lo + V7X_MXU_COLS] = (
            proj(SC_WIDTH + lo, V7X_MXU_COLS) * proj(2 * SC_WIDTH + lo, V7X_MXU_COLS))

    def mixer_a_output():
        conv_a = ka_ref[0:1, :] * cx_scr[ha - 2:ha - 2 + tm, :]
        conv_a = conv_a + ka_ref[1:2, :] * cx_scr[ha - 1:ha - 1 + tm, :]
        conv_a = conv_a + ka_ref[2:3, :] * cx_scr[ha:ha + tm, :]
        y_a = (jnp.concatenate(a_b, axis=1) * conv_a).astype(BF16)
        return jnp.dot(y_a, w_out_ref[0:SC_WIDTH, :], preferred_element_type=F32)

    out_a = None
    for idx, r0 in enumerate(range(0, tm, CONV_ROWS)):
        conv_b_rows(r0)
        if idx < SC_WIDTH // V7X_MXU_COLS:
            mixer_a_cols(idx * V7X_MXU_COLS)
        elif out_a is None:
            out_a = mixer_a_output()

    u = cv_scr[...] + bb_ref[...]
    mu = jnp.mean(u, axis=-1, keepdims=True)
    uc = u - mu
    un = uc * lax.rsqrt(jnp.mean(uc * uc, axis=-1, keepdims=True) + LN_EPS)
    un = un * lng_ref[...] + lnb_ref[...]
    y_b = (un * _sigmoid(un)).astype(BF16)
    o_ref[0] = x + out_a + jnp.dot(y_b, w_out_ref[SC_WIDTH:SC_WIDTH + CM_WIDTH, :],
                                   preferred_element_type=F32)

    cx_scr[0:ha, :] = cx_scr[tm:tm + ha, :]
    u_scr[0:hb, :] = u_scr[tm:tm + hb, :]


def _conv_mixer(x, g, w_in, ka, kb, bb, lng, lnb, w_out, *, layer, w_layer):
    bsz, seq, _ = x.shape
    tm = TOKEN_TILE
    est = (4 * tm * D_MODEL * 4 + (D_MODEL * IN_AB + D_MODEL * D_MODEL) * 2
           + (3 * tm + 64) * SC_WIDTH * 4 + tm * D_MODEL * 2
           + 8 * tm * D_MODEL * 4)
    tile = pl.BlockSpec((1, tm, D_MODEL), lambda b, t: (b, t, 0))
    return pl.pallas_call(
        functools.partial(_conv_mixer_kernel, tm=tm),
        out_shape=jax.ShapeDtypeStruct(x.shape, F32),
        grid=(bsz, seq // tm),
        in_specs=[
            tile,
            _resident((1, D_MODEL), layer),
            _resident(w_in.shape[1:], w_layer),
            _resident(ka.shape[1:], w_layer),
            _resident(kb.shape[1:], w_layer),
            _resident((1, CM_WIDTH), w_layer),
            _resident((1, CM_WIDTH), w_layer),
            _resident((1, CM_WIDTH), w_layer),
            _resident(w_out.shape[1:], w_layer),
        ],
        out_specs=tile,
        scratch_shapes=[
            pltpu.VMEM((CONV_A_HALO + tm, SC_WIDTH), F32),
            pltpu.VMEM((CONV_B_HALO + tm, CM_WIDTH), F32),
            pltpu.VMEM((tm, CM_WIDTH), F32),
        ],
        compiler_params=pltpu.CompilerParams(
            dimension_semantics=("arbitrary", "arbitrary"),
            vmem_limit_bytes=_vmem_limit(est)),
        name="conv_mixer",
    )(x, g, w_in, ka, kb, bb, lng, lnb, w_out)


def _attention_kernel(slopes_ref, q_ref, k_ref, v_ref, o_ref, m_scr, l_scr, *, seq):
    blk, grp, ncls = ATTN_BLOCK, ATTN_GROUP, ATTN_CLASSES
    sub = blk // ncls
    pair = pl.program_id(1)
    slope = [slopes_ref[HEADS_PER_STEP * pair + h] for h in range(HEADS_PER_STEP)]
    first_head = lax.broadcasted_iota(jnp.int32, (1, V7X_LANES), 1) < HEAD_DIM
    head_lanes = [first_head, jnp.logical_not(first_head)]
    nt_dims = (((1,), (1,)), ((), ()))
    q_scale = LOG2_E / math.sqrt(HEAD_DIM)

    q_cls = [q_ref.at[c, 0] for c in range(ncls)]
    k_cls = [k_ref.at[c, 0] for c in range(ncls)]
    v_cls = [v_ref.at[c, 0] for c in range(ncls)]
    o_cls = [o_ref.at[c, 0] for c in range(ncls)]
    m_cls = [m_scr.at[c] for c in range(ncls)]
    l_cls = [l_scr.at[c] for c in range(ncls)]

    def class_rows(c, j):
        def load(refs, first, n):
            start = pl.multiple_of((j * grp + first) * blk, blk)
            return refs[c][pl.ds(start, n * blk), :]

        def store(refs, value):
            refs[c][pl.ds(pl.multiple_of(j * grp * blk, blk), grp * blk), :] = value
        return load, store

    def subclass_rows(c, a, step):
        def load(refs, first, n):
            assert first == 0 and n == grp
            return refs[c][pl.ds(a, grp * blk, stride=step), :]

        def store(refs, value):
            refs[c][pl.ds(a, grp * blk, stride=step), :] = value
        return load, store

    def natural_rows(j):
        def load(refs, first, n):
            start = pl.multiple_of((j * grp + first) * sub, sub)
            per_class = [refs[c][pl.ds(start, n * sub), :] for c in range(ncls)]
            return jnp.concatenate(
                [per_class[c][b * sub:(b + 1) * sub] for b in range(n) for c in range(ncls)], axis=0)

        def store(refs, value):
            start = pl.multiple_of(j * grp * sub, sub)
            for c in range(ncls):
                refs[c][pl.ds(start, grp * sub), :] = jnp.concatenate(
                    [value[b * blk + c * sub:b * blk + (c + 1) * sub] for b in range(grp)], axis=0)
        return load, store

    def run_group(access, first_group, bias_one, bias_two, first_branch, last_branch):
        load, store = access
        q = load(q_cls, 0, grp) * q_scale
        k_first, k_blocks = (0, grp) if first_group else (-1, grp + 1)
        kb = load(k_cls, k_first, k_blocks).astype(BF16)
        vb = load(v_cls, k_first, k_blocks).astype(BF16)
        vb = jnp.concatenate([vb, jnp.ones(vb.shape, BF16)], axis=1)
        qh = [jnp.where(head_lanes[h], q, 0.0).astype(BF16) for h in range(HEADS_PER_STEP)]

        ms, ls, pvs = [], [], []
        for u in range(grp):
            if first_group and u == 0:
                keys, bias = slice(0, blk), bias_one
            else:
                lo = u - 1 if first_group else u
                keys, bias = slice(lo * blk, (lo + 2) * blk), bias_two
            rows = slice(u * blk, (u + 1) * blk)
            lhs = jnp.concatenate([qh[h][rows] for h in range(HEADS_PER_STEP)], axis=0)
            s = lax.dot_general(lhs, kb[keys], nt_dims, preferred_element_type=F32) + bias
            m = jnp.max(s, axis=-1, keepdims=True)
            p = jnp.exp2(s - m).astype(BF16)
            r = jnp.dot(p, vb[keys], preferred_element_type=F32)
            mb = jnp.broadcast_to(m, (HEADS_PER_STEP * blk, V7X_LANES))
            ms.append(jnp.where(first_head, mb[:blk], mb[blk:]))
            pvs.append(jnp.where(first_head, r[:blk, :V7X_LANES], r[blk:, :V7X_LANES]))
            ls.append(jnp.where(first_head, r[:blk, V7X_LANES:], r[blk:, V7X_LANES:]))
        m_new = jnp.concatenate(ms, axis=0)
        l_new = jnp.concatenate(ls, axis=0)
        acc = jnp.concatenate(pvs, axis=0)

        if not first_branch:
            m_old, l_old, acc_old = load(m_cls, 0, grp), load(l_cls, 0, grp), load(o_cls, 0, grp)
            top = jnp.maximum(m_old, m_new)
            w_old = jnp.exp2(m_old - top)
            w_new = jnp.exp2(m_new - top)
            l_new = w_old * l_old + w_new * l_new
            acc = w_old * acc_old + w_new * acc
            m_new = top
        if last_branch:
            store(o_cls, acc / l_new)
        else:
            store(m_cls, m_new)
            store(l_cls, l_new)
            store(o_cls, acc)

    def bias_tables(dil, natural):
        def position(idx):
            if not natural:
                return idx
            inside = idx & (blk - 1)
            return (idx - inside) + ncls * (inside & (sub - 1)) + (inside >> int(math.log2(sub)))
        iq = position(lax.broadcasted_iota(jnp.int32, (blk, 2 * blk), 0))
        ik = position(lax.broadcasted_iota(jnp.int32, (blk, 2 * blk), 1))
        rel = blk + iq - ik
        valid = (rel >= 0) & (rel <= ATTN_BLOCK)
        dist = (dil * rel).astype(F32)
        two = jnp.concatenate(
            [jnp.where(valid, (-LOG2_E * slope[h]) * dist, NEG_INF) for h in range(HEADS_PER_STEP)],
            axis=0)
        return two[:, blk:], two

    rows_per_class = seq // ncls
    for branch, (window, dil) in enumerate(DILATED_BRANCHES):
        assert window // dil == ATTN_BLOCK
        first_branch, last_branch = branch == 0, branch == len(DILATED_BRANCHES) - 1
        bias_one, bias_two = bias_tables(dil, natural=(dil == 1))
        run = functools.partial(run_group, bias_one=bias_one, bias_two=bias_two,
                                first_branch=first_branch, last_branch=last_branch)

        def sweep(make_access, n_groups, run=run):
            run(make_access(0), True)
            if n_groups > 1:
                def later(j, carry):
                    run(make_access(j), False)
                    return carry
                lax.fori_loop(1, n_groups, later, 0, unroll=ATTN_UNROLL[dil])

        if dil == 1:
            sweep(natural_rows, seq // (grp * blk))
        elif dil == ncls:
            for c in range(ncls):
                sweep(functools.partial(class_rows, c), rows_per_class // (grp * blk))
        else:
            step = dil // ncls
            assert rows_per_class // step == grp * blk
            for c in range(ncls):
                def subclass(a, carry, c=c, step=step, run=run):
                    run(subclass_rows(c, a, step), True)
                    return carry
                lax.fori_loop(0, step, subclass, 0, unroll=ATTN_UNROLL[dil])


def _attention(slopes, qkv):
    ncls, bsz, rows, _ = qkv.shape
    n_pairs = N_HEADS // HEADS_PER_STEP
    block_bytes = ncls * rows * V7X_LANES * 4
    est = (2 * 4 * block_bytes
           + 2 * block_bytes
           + 6 * 1024 * 1024)

    def cols(offset):
        return pl.BlockSpec((ncls, 1, rows, V7X_LANES), lambda b, p: (0, b, 0, offset + p))

    return pl.pallas_call(
        functools.partial(_attention_kernel, seq=ncls * rows),
        out_shape=jax.ShapeDtypeStruct((ncls, bsz, rows, ATTN_WIDTH), F32),
        grid=(bsz, n_pairs),
        in_specs=[
            pl.BlockSpec(memory_space=pltpu.SMEM),
            cols(0), cols(n_pairs), cols(2 * n_pairs),
        ],
        out_specs=cols(0),
        scratch_shapes=[pltpu.VMEM((ncls, rows, V7X_LANES), F32),
                        pltpu.VMEM((ncls, rows, V7X_LANES), F32)],
        compiler_params=pltpu.CompilerParams(
            dimension_semantics=("arbitrary", "arbitrary"),
            vmem_limit_bytes=_vmem_limit(est)),
        name="dilated_attention",
    )(slopes, qkv, qkv, qkv)


def _alibi_slopes():
    return np.array([2.0 ** (-8.0 * (i + 1) / N_HEADS) for i in range(N_HEADS)], dtype=np.float32)


def kernel(x, ffn1_norm, ffn1_w_gate_up, ffn1_w_down, mix_norm, ffn2_norm, ffn2_w_gate_up,
           ffn2_w_down, conv_w_in, conv_a_kernel, conv_b_kernel, conv_b_bias, conv_b_ln_gain,
           conv_b_ln_bias, conv_w_out, attn_w_qkv, attn_w_o, final_norm):
    bsz, seq, d = x.shape
    n_tok = bsz * seq
    rows = lambda v: v.reshape(v.shape[0], 1, v.shape[1])
    bf16 = lambda w: w.astype(BF16)
    slopes = jnp.asarray(_alibi_slopes())
    g_final = final_norm.reshape(1, -1)
    ffn1_norm, mix_norm, ffn2_norm = rows(ffn1_norm), rows(mix_norm), rows(ffn2_norm)
    conv_b_bias, conv_b_ln_gain, conv_b_ln_bias = (
        rows(conv_b_bias), rows(conv_b_ln_gain), rows(conv_b_ln_bias))
    ffn1_wgu, ffn1_wd = ffn1_w_gate_up, ffn1_w_down
    ffn2_wgu, ffn2_wd = ffn2_w_gate_up, ffn2_w_down
    conv_w_in, conv_w_out = bf16(conv_w_in), bf16(conv_w_out)
    attn_w_qkv, attn_w_o = bf16(attn_w_qkv), bf16(attn_w_o)

    for layer in range(DEPTH):
        x = _ffn(x.reshape(n_tok, d), ffn1_norm, ffn1_wgu, ffn1_wd, g_final,
                 layer=layer, final_norm=False).reshape(bsz, seq, d)
        i = layer // 2
        last = layer == DEPTH - 1
        if layer % 2 == 0:
            x = _conv_mixer(x, mix_norm, conv_w_in, conv_a_kernel, conv_b_kernel, conv_b_bias,
                            conv_b_ln_gain, conv_b_ln_bias, conv_w_out, layer=layer, w_layer=i)
            x = _ffn(x.reshape(n_tok, d), ffn2_norm, ffn2_wgu, ffn2_wd, g_final,
                     layer=layer, final_norm=last).reshape(bsz, seq, d)
        else:
            qkv = _norm_proj(x, mix_norm, attn_w_qkv, layer=layer, w_layer=i)
            o = _attention(slopes, qkv)
            x = _attn_out_ffn(x, o, attn_w_o, ffn2_norm, ffn2_wgu, ffn2_wd, g_final,
                              layer=layer, w_layer=i, final_norm=last)
    return x
```

```python
import functools
import math

import numpy as np
import jax
import jax.numpy as jnp
from jax import lax
from jax.experimental import pallas as pl
from jax.experimental.pallas import tpu as pltpu

D_MODEL = 1024
DEPTH = 4
D_FF = 2816
RMS_EPS = 1e-6
LN_EPS = 1e-5
SC_WIDTH = 512
SC_KERNEL = 3
CM_WIDTH = 512
CM_KERNEL = 31
IN_AB = 3 * SC_WIDTH + 2 * CM_WIDTH
N_HEADS = 16
HEAD_DIM = 64
ATTN_WIDTH = N_HEADS * HEAD_DIM
DILATED_BRANCHES = ((128, 1), (512, 4), (2048, 16))
ATTN_BLOCK = 128
NEG_INF = -1e30

V7X_LANES = 128
V7X_SUBLANES = 8
V7X_MXU_COLS = 256
V7X_VMEM_BYTES = 64 * 1024 * 1024

F32 = jnp.float32
BF16 = jnp.bfloat16

TOKEN_TILE = 512
FFN_TILE = 1024
FF_CHUNK = V7X_MXU_COLS
CONV_ROWS = 64
CONV_A_HALO = V7X_SUBLANES
CONV_B_HALO = 32
HEADS_PER_STEP = V7X_LANES // HEAD_DIM
ATTN_GROUP = 4
ATTN_CLASSES = 4
ATTN_UNROLL = {1: 5, 4: 3, 16: 2}
LOG2_E = math.log2(math.e)


def _vmem_limit(estimate_bytes):
    return int(min(V7X_VMEM_BYTES - 8 * 1024 * 1024, estimate_bytes))


def _rms_norm(x, g):
    return x * lax.rsqrt(jnp.mean(x * x, axis=-1, keepdims=True) + RMS_EPS) * g


def _sigmoid(x):
    return 1.0 / (1.0 + jnp.exp(-x))


def _resident(shape, layer=None):
    zeros = (0,) * len(shape)
    if layer is None:
        return pl.BlockSpec(shape, lambda *_: zeros, pipeline_mode=pl.Buffered(1))
    return pl.BlockSpec((None,) + tuple(shape), lambda *_: (layer,) + zeros,
                        pipeline_mode=pl.Buffered(1))


def _ffn_tile(x, g_ref, wgu_ref, wd_ref, gf_ref, a_scr, final_norm, before_chunk=None):
    h = _rms_norm(x, g_ref[...]).astype(BF16)
    for c in range(D_FF // FF_CHUNK):
        lo = c * FF_CHUNK
        if before_chunk is not None:
            before_chunk(c)
        gate = jnp.dot(h, wgu_ref[:, lo:lo + FF_CHUNK], preferred_element_type=F32)
        up = jnp.dot(h, wgu_ref[:, D_FF + lo:D_FF + lo + FF_CHUNK], preferred_element_type=F32)
        a_scr[:, lo:lo + FF_CHUNK] = (gate * _sigmoid(gate) * up).astype(BF16)
    y = x + 0.5 * jnp.dot(a_scr[...], wd_ref[...], preferred_element_type=F32)
    if final_norm:
        y = _rms_norm(y, gf_ref[...])
    return y


def _ffn_weight_scratch():
    stage = lambda shape: pltpu.VMEM((2,) + shape, F32)
    return [pltpu.VMEM((D_MODEL, 2 * D_FF), BF16),
            pltpu.VMEM((D_FF, D_MODEL), BF16),
            stage((D_MODEL, FF_CHUNK)), stage((D_MODEL, FF_CHUNK)), stage((FF_CHUNK, D_MODEL)),
            pltpu.SemaphoreType.DMA((2, 3))]


def _ffn_weight_streamer(wgu_hbm, wd_hbm, scratch):
    wgu_bf, wd_bf, stage_g, stage_u, stage_d, sems = scratch
    n_chunks = D_FF // FF_CHUNK

    def parts(c):
        slot, lo = c % 2, c * FF_CHUNK
        gate_cols, up_cols = pl.ds(lo, FF_CHUNK), pl.ds(D_FF + lo, FF_CHUNK)
        down_rows = pl.ds(lo, FF_CHUNK)
        return [
            (pltpu.make_async_copy(wgu_hbm.at[:, gate_cols], stage_g.at[slot], sems.at[slot, 0]),
             stage_g, wgu_bf.at[:, gate_cols]),
            (pltpu.make_async_copy(wgu_hbm.at[:, up_cols], stage_u.at[slot], sems.at[slot, 1]),
             stage_u, wgu_bf.at[:, up_cols]),
            (pltpu.make_async_copy(wd_hbm.at[down_rows, :], stage_d.at[slot], sems.at[slot, 2]),
             stage_d, wd_bf.at[down_rows, :]),
        ]

    def fetch(c):
        for copy, _, _ in parts(c):
            copy.start()

    def before_chunk(c):
        if c == 0:
            fetch(0)
        if c + 1 < n_chunks:
            fetch(c + 1)
        for copy, stage, dst in parts(c):
            copy.wait()
            dst[...] = stage[c % 2].astype(BF16)

    return before_chunk


def _ffn_tile_streaming(x_ref_tile, o_ref_tile, first_step, wgu_hbm, wd_hbm, scratch,
                        g_ref, gf_ref, a_scr, final_norm):
    wgu_bf, wd_bf = scratch[0], scratch[1]

    @pl.when(first_step)
    def _with_weight_load():
        o_ref_tile(_ffn_tile(x_ref_tile(), g_ref, wgu_bf, wd_bf, gf_ref, a_scr, final_norm,
                             before_chunk=_ffn_weight_streamer(wgu_hbm, wd_hbm, scratch)))

    @pl.when(jnp.logical_not(first_step))
    def _weights_resident():
        o_ref_tile(_ffn_tile(x_ref_tile(), g_ref, wgu_bf, wd_bf, gf_ref, a_scr, final_norm))


def _ffn_kernel(x_ref, g_ref, wgu_hbm, wd_hbm, gf_ref, o_ref, a_scr, *weight_scratch,
                layer, final_norm):
    def write(y):
        o_ref[...] = y
    _ffn_tile_streaming(lambda: x_ref[...], write, pl.program_id(0) == 0,
                        wgu_hbm.at[layer], wd_hbm.at[layer], weight_scratch,
                        g_ref, gf_ref, a_scr, final_norm)


def _ffn_vmem_estimate(tm):
    return (4 * tm * D_MODEL * 4
            + 3 * D_MODEL * D_FF * 2
            + 2 * 3 * D_MODEL * FF_CHUNK * 4
            + tm * D_FF * 2
            + 6 * tm * D_MODEL * 4)


def _ffn(x2d, g, wgu, wd, g_final, *, layer, final_norm):
    n_tok = x2d.shape[0]
    tm = FFN_TILE
    est = _ffn_vmem_estimate(tm)
    return pl.pallas_call(
        functools.partial(_ffn_kernel, layer=layer, final_norm=final_norm),
        out_shape=jax.ShapeDtypeStruct((n_tok, D_MODEL), F32),
        grid=(n_tok // tm,),
        in_specs=[
            pl.BlockSpec((tm, D_MODEL), lambda i: (i, 0)),
            _resident((1, D_MODEL), layer),
            pl.BlockSpec(memory_space=pl.ANY),
            pl.BlockSpec(memory_space=pl.ANY),
            _resident((1, D_MODEL)),
        ],
        out_specs=pl.BlockSpec((tm, D_MODEL), lambda i: (i, 0)),
        scratch_shapes=[pltpu.VMEM((tm, D_FF), BF16)] + _ffn_weight_scratch(),
        compiler_params=pltpu.CompilerParams(
            dimension_semantics=("arbitrary",), vmem_limit_bytes=_vmem_limit(est)),
        name="ffn_final" if final_norm else "ffn",
    )(x2d, g, wgu, wd, g_final)


def _class_rows(c, tm):
    return pl.ds(c, tm // ATTN_CLASSES, stride=ATTN_CLASSES)


def _lane_block(j):
    return slice(j * V7X_LANES, (j + 1) * V7X_LANES)


def _to_class_major(tile, scr, tm):
    n_blocks = tile.shape[1] // V7X_LANES
    for j in range(n_blocks):
        scr[j] = tile[:, _lane_block(j)]
    return jnp.concatenate(
        [jnp.concatenate([scr[j, _class_rows(c, tm), :] for j in range(n_blocks)], axis=1)
         for c in range(ATTN_CLASSES)], axis=0)


def _from_class_major(tile, scr, tm):
    n_blocks = tile.shape[1] // V7X_LANES
    per_class = tm // ATTN_CLASSES
    for j in range(n_blocks):
        for c in range(ATTN_CLASSES):
            scr[j, _class_rows(c, tm), :] = tile[c * per_class:(c + 1) * per_class, _lane_block(j)]
    return jnp.concatenate([scr[j] for j in range(n_blocks)], axis=1)


def _norm_proj_kernel(x_ref, g_ref, w_ref, o_ref, scr, *, tm):
    per_class = tm // ATTN_CLASSES
    h = _rms_norm(_to_class_major(x_ref[0], scr, tm), g_ref[...]).astype(BF16)
    y = jnp.dot(h, w_ref[...], preferred_element_type=F32)
    for c in range(ATTN_CLASSES):
        o_ref[c, 0] = y[c * per_class:(c + 1) * per_class]


def _norm_proj(x, g, w, *, layer, w_layer):
    bsz, seq, _ = x.shape
    n_out = w.shape[-1]
    tm = TOKEN_TILE
    per_class = tm // ATTN_CLASSES
    est = (2 * tm * D_MODEL * 4 + 2 * tm * n_out * 4 + D_MODEL * n_out * 2 + 2 * tm * n_out * 4)
    return pl.pallas_call(
        functools.partial(_norm_proj_kernel, tm=tm),
        out_shape=jax.ShapeDtypeStruct((ATTN_CLASSES, bsz, seq // ATTN_CLASSES, n_out), F32),
        grid=(bsz, seq // tm),
        in_specs=[
            pl.BlockSpec((1, tm, D_MODEL), lambda b, i: (b, i, 0)),
            _resident((1, D_MODEL), layer),
            _resident(w.shape[1:], w_layer),
        ],
        out_specs=pl.BlockSpec((ATTN_CLASSES, 1, per_class, n_out), lambda b, i: (0, b, i, 0)),
        scratch_shapes=[pltpu.VMEM((D_MODEL // V7X_LANES, tm, V7X_LANES), F32)],
        compiler_params=pltpu.CompilerParams(
            dimension_semantics=("arbitrary",) * 2, vmem_limit_bytes=_vmem_limit(est)),
        name="norm_proj",
    )(x, g, w)


def _attn_out_ffn_kernel(x_ref, y_ref, wo_ref, g_ref, wgu_hbm, wd_hbm, gf_ref, o_ref,
                         a_scr, cm_scr, *weight_scratch, tm, layer, final_norm):
    first_step = (pl.program_id(0) == 0) & (pl.program_id(1) == 0)

    def read():
        y = jnp.concatenate([y_ref[c, 0] for c in range(ATTN_CLASSES)], axis=0).astype(BF16)
        r = jnp.dot(y, wo_ref[...], preferred_element_type=F32)
        return x_ref[0] + _from_class_major(r, cm_scr, tm)

    def write(y):
        o_ref[0] = y
    _ffn_tile_streaming(read, write, first_step, wgu_hbm.at[layer], wd_hbm.at[layer],
                        weight_scratch, g_ref, gf_ref, a_scr, final_norm)


def _attn_out_ffn(x, y, w_o, g, wgu, wd, g_final, *, layer, w_layer, final_norm):
    bsz, seq, _ = x.shape
    tm = TOKEN_TILE
    per_class = tm // ATTN_CLASSES
    est = (_ffn_vmem_estimate(tm) + ATTN_WIDTH * D_MODEL * 2
           + 2 * tm * ATTN_WIDTH * 4
           + 3 * tm * D_MODEL * 4)
    x_tile = pl.BlockSpec((1, tm, D_MODEL), lambda b, i: (b, i, 0))
    return pl.pallas_call(
        functools.partial(_attn_out_ffn_kernel, tm=tm, layer=layer, final_norm=final_norm),
        out_shape=jax.ShapeDtypeStruct(x.shape, F32),
        grid=(bsz, seq // tm),
        in_specs=[
            x_tile,
            pl.BlockSpec((ATTN_CLASSES, 1, per_class, y.shape[-1]), lambda b, i: (0, b, i, 0)),
            _resident(w_o.shape[1:], w_layer),
            _resident((1, D_MODEL), layer),
            pl.BlockSpec(memory_space=pl.ANY),
            pl.BlockSpec(memory_space=pl.ANY),
            _resident((1, D_MODEL)),
        ],
        out_specs=x_tile,
        scratch_shapes=[pltpu.VMEM((tm, D_FF), BF16),
                        pltpu.VMEM((D_MODEL // V7X_LANES, tm, V7X_LANES), F32)]
        + _ffn_weight_scratch(),
        compiler_params=pltpu.CompilerParams(
            dimension_semantics=("arbitrary",) * 2, vmem_limit_bytes=_vmem_limit(est)),
        name="attn_out_ffn_final" if final_norm else "attn_out_ffn",
    )(x, y, w_o, g, wgu, wd, g_final)


def _conv_mixer_kernel(x_ref, g_ref, w_in_ref, ka_ref, kb_ref, bb_ref, lng_ref, lnb_ref,
                       w_out_ref, o_ref, cx_scr, u_scr, cv_scr, *, tm):
    ha, hb = CONV_A_HALO, CONV_B_HALO

    @pl.when(pl.program_id(1) == 0)
    def _start_of_sequence():
        cx_scr[0:ha, :] = jnp.zeros((ha, SC_WIDTH), F32)
        u_scr[0:hb, :] = jnp.zeros((hb, CM_WIDTH), F32)

    x = x_ref[0]
    h = _rms_norm(x, g_ref[...]).astype(BF16)

    def proj(lo, width):
        return jnp.dot(h, w_in_ref[:, lo:lo + width], preferred_element_type=F32)

    u_scr[hb:hb + tm, :] = proj(3 * SC_WIDTH, CM_WIDTH) * _sigmoid(
        proj(3 * SC_WIDTH + CM_WIDTH, CM_WIDTH))

    first_off = hb - (CM_KERNEL - 1)

    def conv_b_rows(r0):
        for j in range(CM_WIDTH // V7X_LANES):
            lanes = slice(j * V7X_LANES, (j + 1) * V7X_LANES)
            n_win = CONV_ROWS + hb
            window = u_scr[pl.ds(r0, n_win), lanes]
            acc = jnp.zeros((CONV_ROWS, V7X_LANES), F32)
            for sub in range(V7X_SUBLANES):
                offs = [o for o in range(first_off, hb + 1) if o % V7X_SUBLANES == sub]
                shifted = pltpu.roll(window, n_win - sub, axis=0) if sub else window
                for o in offs:
                    k = o - first_off
                    acc = acc + kb_ref[k:k + 1, lanes] * shifted[o - sub:o - sub + CONV_ROWS, :]
            cv_scr[pl.ds(r0, CONV_ROWS), lanes] = acc

    a_b = []

    def mixer_a_cols(lo):
        a_b.append(proj(lo, V7X_MXU_COLS))
        cx_scr[ha:ha + tm, lo:lo + V7X_MXU_COLS] = (
            proj(SC_WIDTH + lo, V7X_MXU_COLS) * proj(2 * SC_WIDTH + lo, V7X_MXU_COLS))

    def mixer_a_output():
        conv_a = ka_ref[0:1, :] * cx_scr[ha - 2:ha - 2 + tm, :]
        conv_a = conv_a + ka_ref[1:2, :] * cx_scr[ha - 1:ha - 1 + tm, :]
        conv_a = conv_a + ka_ref[2:3, :] * cx_scr[ha:ha + tm, :]
        y_a = (jnp.concatenate(a_b, axis=1) * conv_a).astype(BF16)
        return jnp.dot(y_a, w_out_ref[0:SC_WIDTH, :], preferred_element_type=F32)

    out_a = None
    for idx, r0 in enumerate(range(0, tm, CONV_ROWS)):
        conv_b_rows(r0)
        if idx < SC_WIDTH // V7X_MXU_COLS:
            mixer_a_cols(idx * V7X_MXU_COLS)
        elif out_a is None:
            out_a = mixer_a_output()

    u = cv_scr[...] + bb_ref[...]
    mu = jnp.mean(u, axis=-1, keepdims=True)
    uc = u - mu
    un = uc * lax.rsqrt(jnp.mean(uc * uc, axis=-1, keepdims=True) + LN_EPS)
    un = un * lng_ref[...] + lnb_ref[...]
    y_b = (un * _sigmoid(un)).astype(BF16)
    o_ref[0] = x + out_a + jnp.dot(y_b, w_out_ref[SC_WIDTH:SC_WIDTH + CM_WIDTH, :],
                                   preferred_element_type=F32)

    cx_scr[0:ha, :] = cx_scr[tm:tm + ha, :]
    u_scr[0:hb, :] = u_scr[tm:tm + hb, :]


def _conv_mixer(x, g, w_in, ka, kb, bb, lng, lnb, w_out, *, layer, w_layer):
    bsz, seq, _ = x.shape
    tm = TOKEN_TILE
    est = (4 * tm * D_MODEL * 4 + (D_MODEL * IN_AB + D_MODEL * D_MODEL) * 2
           + (3 * tm + 64) * SC_WIDTH * 4 + tm * D_MODEL * 2
           + 8 * tm * D_MODEL * 4)
    tile = pl.BlockSpec((1, tm, D_MODEL), lambda b, t: (b, t, 0))
    return pl.pallas_call(
        functools.partial(_conv_mixer_kernel, tm=tm),
        out_shape=jax.ShapeDtypeStruct(x.shape, F32),
        grid=(bsz, seq // tm),
        in_specs=[
            tile,
            _resident((1, D_MODEL), layer),
            _resident(w_in.shape[1:], w_layer),
            _resident(ka.shape[1:], w_layer),
            _resident(kb.shape[1:], w_layer),
            _resident((1, CM_WIDTH), w_layer),
            _resident((1, CM_WIDTH), w_layer),
            _resident((1, CM_WIDTH), w_layer),
            _resident(w_out.shape[1:], w_layer),
        ],
        out_specs=tile,
        scratch_shapes=[
            pltpu.VMEM((CONV_A_HALO + tm, SC_WIDTH), F32),
            pltpu.VMEM((CONV_B_HALO + tm, CM_WIDTH), F32),
            pltpu.VMEM((tm, CM_WIDTH), F32),
        ],
        compiler_params=pltpu.CompilerParams(
            dimension_semantics=("arbitrary", "arbitrary"),
            vmem_limit_bytes=_vmem_limit(est)),
        name="conv_mixer",
    )(x, g, w_in, ka, kb, bb, lng, lnb, w_out)


def _attention_kernel(slopes_ref, q_ref, k_ref, v_ref, o_ref, m_scr, l_scr, *, seq):
    blk, grp, ncls = ATTN_BLOCK, ATTN_GROUP, ATTN_CLASSES
    sub = blk // ncls
    pair = pl.program_id(1)
    slope = [slopes_ref[HEADS_PER_STEP * pair + h] for h in range(HEADS_PER_STEP)]
    first_head = lax.broadcasted_iota(jnp.int32, (1, V7X_LANES), 1) < HEAD_DIM
    head_lanes = [first_head, jnp.logical_not(first_head)]
    nt_dims = (((1,), (1,)), ((), ()))
    q_scale = LOG2_E / math.sqrt(HEAD_DIM)

    q_cls = [q_ref.at[c, 0] for c in range(ncls)]
    k_cls = [k_ref.at[c, 0] for c in range(ncls)]
    v_cls = [v_ref.at[c, 0] for c in range(ncls)]
    o_cls = [o_ref.at[c, 0] for c in range(ncls)]
    m_cls = [m_scr.at[c] for c in range(ncls)]
    l_cls = [l_scr.at[c] for c in range(ncls)]

    def class_rows(c, j):
        def load(refs, first, n):
            start = pl.multiple_of((j * grp + first) * blk, blk)
            return refs[c][pl.ds(start, n * blk), :]

        def store(refs, value):
            refs[c][pl.ds(pl.multiple_of(j * grp * blk, blk), grp * blk), :] = value
        return load, store

    def subclass_rows(c, a, step):
        def load(refs, first, n):
            assert first == 0 and n == grp
            return refs[c][pl.ds(a, grp * blk, stride=step), :]

        def store(refs, value):
            refs[c][pl.ds(a, grp * blk, stride=step), :] = value
        return load, store

    def natural_rows(j):
        def load(refs, first, n):
            start = pl.multiple_of((j * grp + first) * sub, sub)
            per_class = [refs[c][pl.ds(start, n * sub), :] for c in range(ncls)]
            return jnp.concatenate(
                [per_class[c][b * sub:(b + 1) * sub] for b in range(n) for c in range(ncls)], axis=0)

        def store(refs, value):
            start = pl.multiple_of(j * grp * sub, sub)
            for c in range(ncls):
                refs[c][pl.ds(start, grp * sub), :] = jnp.concatenate(
                    [value[b * blk + c * sub:b * blk + (c + 1) * sub] for b in range(grp)], axis=0)
        return load, store

    def run_group(access, first_group, bias_one, bias_two, first_branch, last_branch):
        load, store = access
        q = load(q_cls, 0, grp) * q_scale
        k_first, k_blocks = (0, grp) if first_group else (-1, grp + 1)
        kb = load(k_cls, k_first, k_blocks).astype(BF16)
        vb = load(v_cls, k_first, k_blocks).astype(BF16)
        vb = jnp.concatenate([vb, jnp.ones(vb.shape, BF16)], axis=1)
        qh = [jnp.where(head_lanes[h], q, 0.0).astype(BF16) for h in range(HEADS_PER_STEP)]

        ms, ls, pvs = [], [], []
        for u in range(grp):
            if first_group and u == 0:
                keys, bias = slice(0, blk), bias_one
            else:
                lo = u - 1 if first_group else u
                keys, bias = slice(lo * blk, (lo + 2) * blk), bias_two
            rows = slice(u * blk, (u + 1) * blk)
            lhs = jnp.concatenate([qh[h][rows] for h in range(HEADS_PER_STEP)], axis=0)
            s = lax.dot_general(lhs, kb[keys], nt_dims, preferred_element_type=F32) + bias
            m = jnp.max(s, axis=-1, keepdims=True)
            p = jnp.exp2(s - m).astype(BF16)
            r = jnp.dot(p, vb[keys], preferred_element_type=F32)
            mb = jnp.broadcast_to(m, (HEADS_PER_STEP * blk, V7X_LANES))
            ms.append(jnp.where(first_head, mb[:blk], mb[blk:]))
            pvs.append(jnp.where(first_head, r[:blk, :V7X_LANES], r[blk:, :V7X_LANES]))
            ls.append(jnp.where(first_head, r[:blk, V7X_LANES:], r[blk:, V7X_LANES:]))
        m_new = jnp.concatenate(ms, axis=0)
        l_new = jnp.concatenate(ls, axis=0)
        acc = jnp.concatenate(pvs, axis=0)

        if not first_branch:
            m_old, l_old, acc_old = load(m_cls, 0, grp), load(l_cls, 0, grp), load(o_cls, 0, grp)
            top = jnp.maximum(m_old, m_new)
            w_old = jnp.exp2(m_old - top)
            w_new = jnp.exp2(m_new - top)
            l_new = w_old * l_old + w_new * l_new
            acc = w_old * acc_old + w_new * acc
            m_new = top
        if last_branch:
            store(o_cls, acc / l_new)
        else:
            store(m_cls, m_new)
            store(l_cls, l_new)
            store(o_cls, acc)

    def bias_tables(dil, natural):
        def position(idx):
            if not natural:
                return idx
            inside = idx & (blk - 1)
            return (idx - inside) + ncls * (inside & (sub - 1)) + (inside >> int(math.log2(sub)))
        iq = position(lax.broadcasted_iota(jnp.int32, (blk, 2 * blk), 0))
        ik = position(lax.broadcasted_iota(jnp.int32, (blk, 2 * blk), 1))
        rel = blk + iq - ik
        valid = (rel >= 0) & (rel <= ATTN_BLOCK)
        dist = (dil * rel).astype(F32)
        two = jnp.concatenate(
            [jnp.where(valid, (-LOG2_E * slope[h]) * dist, NEG_INF) for h in range(HEADS_PER_STEP)],
            axis=0)
        return two[:, blk:], two

    rows_per_class = seq // ncls
    for branch, (window, dil) in enumerate(DILATED_BRANCHES):
        assert window // dil == ATTN_BLOCK
        first_branch, last_branch = branch == 0, branch == len(DILATED_BRANCHES) - 1
        bias_one, bias_two = bias_tables(dil, natural=(dil == 1))
        run = functools.partial(run_group, bias_one=bias_one, bias_two=bias_two,
                                first_branch=first_branch, last_branch=last_branch)

        def sweep(make_access, n_groups, run=run):
            run(make_access(0), True)
            if n_groups > 1:
                def later(j, carry):
                    run(make_access(j), False)
                    return carry
                lax.fori_loop(1, n_groups, later, 0, unroll=ATTN_UNROLL[dil])

        if dil == 1:
            sweep(natural_rows, seq // (grp * blk))
        elif dil == ncls:
            for c in range(ncls):
                sweep(functools.partial(class_rows, c), rows_per_class // (grp * blk))
        else:
            step = dil // ncls
            assert rows_per_class // step == grp * blk
            for c in range(ncls):
                def subclass(a, carry, c=c, step=step, run=run):
                    run(subclass_rows(c, a, step), True)
                    return carry
                lax.fori_loop(0, step, subclass, 0, unroll=ATTN_UNROLL[dil])


def _attention(slopes, qkv):
    ncls, bsz, rows, _ = qkv.shape
    n_pairs = N_HEADS // HEADS_PER_STEP
    block_bytes = ncls * rows * V7X_LANES * 4
    est = (2 * 4 * block_bytes
           + 2 * block_bytes
           + 6 * 1024 * 1024)

    def cols(offset):
        return pl.BlockSpec((ncls, 1, rows, V7X_LANES), lambda b, p: (0, b, 0, offset + p))

    return pl.pallas_call(
        functools.partial(_attention_kernel, seq=ncls * rows),
        out_shape=jax.ShapeDtypeStruct((ncls, bsz, rows, ATTN_WIDTH), F32),
        grid=(bsz, n_pairs),
        in_specs=[
            pl.BlockSpec(memory_space=pltpu.SMEM),
            cols(0), cols(n_pairs), cols(2 * n_pairs),
        ],
        out_specs=cols(0),
        scratch_shapes=[pltpu.VMEM((ncls, rows, V7X_LANES), F32),
                        pltpu.VMEM((ncls, rows, V7X_LANES), F32)],
        compiler_params=pltpu.CompilerParams(
            dimension_semantics=("arbitrary", "arbitrary"),
            vmem_limit_bytes=_vmem_limit(est)),
        name="dilated_attention",
    )(slopes, qkv, qkv, qkv)


def _alibi_slopes():
    return np.array([2.0 ** (-8.0 * (i + 1) / N_HEADS) for i in range(N_HEADS)], dtype=np.float32)


def kernel(x, ffn1_norm, ffn1_w_gate_up, ffn1_w_down, mix_norm, ffn2_norm, ffn2_w_gate_up,
           ffn2_w_down, conv_w_in, conv_a_kernel, conv_b_kernel, conv_b_bias, conv_b_ln_gain,
           conv_b_ln_bias, conv_w_out, attn_w_qkv, attn_w_o, final_norm):
    bsz, seq, d = x.shape
    n_tok = bsz * seq
    rows = lambda v: v.reshape(v.shape[0], 1, v.shape[1])
    bf16 = lambda w: w.astype(BF16)
    slopes = jnp.asarray(_alibi_slopes())
    g_final = final_norm.reshape(1, -1)
    ffn1_norm, mix_norm, ffn2_norm = rows(ffn1_norm), rows(mix_norm), rows(ffn2_norm)
    conv_b_bias, conv_b_ln_gain, conv_b_ln_bias = (
        rows(conv_b_bias), rows(conv_b_ln_gain), rows(conv_b_ln_bias))
    ffn1_wgu, ffn1_wd = ffn1_w_gate_up, ffn1_w_down
    ffn2_wgu, ffn2_wd = ffn2_w_gate_up, ffn2_w_down
    conv_w_in, conv_w_out = bf16(conv_w_in), bf16(conv_w_out)
    attn_w_qkv, attn_w_o = bf16(attn_w_qkv), bf16(attn_w_o)

    for layer in range(DEPTH):
        x = _ffn(x.reshape(n_tok, d), ffn1_norm, ffn1_wgu, ffn1_wd, g_final,
                 layer=layer, final_norm=False).reshape(bsz, seq, d)
        i = layer // 2
        last = layer == DEPTH - 1
        if layer % 2 == 0:
            x = _conv_mixer(x, mix_norm, conv_w_in, conv_a_kernel, conv_b_kernel, conv_b_bias,
                            conv_b_ln_gain, conv_b_ln_bias, conv_w_out, layer=layer, w_layer=i)
            x = _ffn(x.reshape(n_tok, d), ffn2_norm, ffn2_wgu, ffn2_wd, g_final,
                     layer=layer, final_norm=last).reshape(bsz, seq, d)
        else:
            qkv = _norm_proj(x, mix_norm, attn_w_qkv, layer=layer, w_layer=i)
            o = _attention(slopes, qkv)
            x = _attn_out_ffn(x, o, attn_w_o, ffn2_norm, ffn2_wgu, ffn2_wd, g_final,
                              layer=layer, w_layer=i, final_norm=last)
    return x
```

```python
import functools
import math

import numpy as np
import jax
import jax.numpy as jnp
from jax import lax
from jax.experimental import pallas as pl
from jax.experimental.pallas import tpu as pltpu

D_MODEL = 1024
DEPTH = 4
D_FF = 2816
RMS_EPS = 1e-6
LN_EPS = 1e-5
SC_WIDTH = 512
SC_KERNEL = 3
CM_WIDTH = 512
CM_KERNEL = 31
IN_AB = 3 * SC_WIDTH + 2 * CM_WIDTH
N_HEADS = 16
HEAD_DIM = 64
ATTN_WIDTH = N_HEADS * HEAD_DIM
DILATED_BRANCHES = ((128, 1), (512, 4), (2048, 16))
ATTN_BLOCK = 128
NEG_INF = -1e30

V7X_LANES = 128
V7X_SUBLANES = 8
V7X_MXU_COLS = 256
V7X_VMEM_BYTES = 64 * 1024 * 1024

F32 = jnp.float32
BF16 = jnp.bfloat16

TOKEN_TILE = 512
FFN_TILE = 1024
FF_CHUNK = V7X_MXU_COLS
WEIGHT_CHUNK_ROWS = 128
CONV_ROWS = 64
CONV_A_HALO = V7X_SUBLANES
CONV_B_HALO = 32
HEADS_PER_STEP = V7X_LANES // HEAD_DIM
ATTN_GROUP = 4
ATTN_CLASSES = 4
ATTN_UNROLL = {1: 5, 4: 3, 16: 2}
LOG2_E = math.log2(math.e)


def _vmem_limit(estimate_bytes):
    return int(min(V7X_VMEM_BYTES - 8 * 1024 * 1024, estimate_bytes))


def _rms_norm(x, g):
    return x * lax.rsqrt(jnp.mean(x * x, axis=-1, keepdims=True) + RMS_EPS) * g


def _sigmoid(x):
    return 1.0 / (1.0 + jnp.exp(-x))


def _resident(shape, layer=None):
    zeros = (0,) * len(shape)
    if layer is None:
        return pl.BlockSpec(shape, lambda *_: zeros, pipeline_mode=pl.Buffered(1))
    return pl.BlockSpec((None,) + tuple(shape), lambda *_: (layer,) + zeros,
                        pipeline_mode=pl.Buffered(1))


def _weight_scratch(shape):
    return [pltpu.VMEM(shape, BF16), pltpu.VMEM((2, WEIGHT_CHUNK_ROWS, shape[1]), F32),
            pltpu.SemaphoreType.DMA((2,))]


def _load_weight_bf16(src_hbm, dst, stage, sems):
    n_chunks = src_hbm.shape[0] // WEIGHT_CHUNK_ROWS

    def chunk_copy(c):
        rows = pl.ds(c * WEIGHT_CHUNK_ROWS, WEIGHT_CHUNK_ROWS)
        return pltpu.make_async_copy(src_hbm.at[rows, :], stage.at[c % 2], sems.at[c % 2])

    chunk_copy(0).start()
    for c in range(n_chunks):
        if c + 1 < n_chunks:
            chunk_copy(c + 1).start()
        chunk_copy(c).wait()
        dst[c * WEIGHT_CHUNK_ROWS:(c + 1) * WEIGHT_CHUNK_ROWS, :] = stage[c % 2].astype(BF16)


def _ffn_tile(x, g_ref, wgu_ref, wd_ref, gf_ref, a_scr, final_norm, before_chunk=None):
    h = _rms_norm(x, g_ref[...]).astype(BF16)
    for c in range(D_FF // FF_CHUNK):
        lo = c * FF_CHUNK
        if before_chunk is not None:
            before_chunk(c)
        gate = jnp.dot(h, wgu_ref[:, lo:lo + FF_CHUNK], preferred_element_type=F32)
        up = jnp.dot(h, wgu_ref[:, D_FF + lo:D_FF + lo + FF_CHUNK], preferred_element_type=F32)
        a_scr[:, lo:lo + FF_CHUNK] = (gate * _sigmoid(gate) * up).astype(BF16)
    y = x + 0.5 * jnp.dot(a_scr[...], wd_ref[...], preferred_element_type=F32)
    if final_norm:
        y = _rms_norm(y, gf_ref[...])
    return y


def _ffn_weight_scratch():
    stage = lambda shape: pltpu.VMEM((2,) + shape, F32)
    return [pltpu.VMEM((D_MODEL, 2 * D_FF), BF16),
            pltpu.VMEM((D_FF, D_MODEL), BF16),
            stage((D_MODEL, FF_CHUNK)), stage((D_MODEL, FF_CHUNK)), stage((FF_CHUNK, D_MODEL)),
            pltpu.SemaphoreType.DMA((2, 3))]


def _ffn_weight_streamer(wgu_hbm, wd_hbm, scratch):
    wgu_bf, wd_bf, stage_g, stage_u, stage_d, sems = scratch
    n_chunks = D_FF // FF_CHUNK

    def parts(c):
        slot, lo = c % 2, c * FF_CHUNK
        gate_cols, up_cols = pl.ds(lo, FF_CHUNK), pl.ds(D_FF + lo, FF_CHUNK)
        down_rows = pl.ds(lo, FF_CHUNK)
        return [
            (pltpu.make_async_copy(wgu_hbm.at[:, gate_cols], stage_g.at[slot], sems.at[slot, 0]),
             stage_g, wgu_bf.at[:, gate_cols]),
            (pltpu.make_async_copy(wgu_hbm.at[:, up_cols], stage_u.at[slot], sems.at[slot, 1]),
             stage_u, wgu_bf.at[:, up_cols]),
            (pltpu.make_async_copy(wd_hbm.at[down_rows, :], stage_d.at[slot], sems.at[slot, 2]),
             stage_d, wd_bf.at[down_rows, :]),
        ]

    def fetch(c):
        for copy, _, _ in parts(c):
            copy.start()

    def before_chunk(c):
        if c == 0:
            fetch(0)
        if c + 1 < n_chunks:
            fetch(c + 1)
        for copy, stage, dst in parts(c):
            copy.wait()
            dst[...] = stage[c % 2].astype(BF16)

    return before_chunk


def _ffn_tile_streaming(x_ref_tile, o_ref_tile, first_step, wgu_hbm, wd_hbm, scratch,
                        g_ref, gf_ref, a_scr, final_norm):
    wgu_bf, wd_bf = scratch[0], scratch[1]

    @pl.when(first_step)
    def _with_weight_load():
        o_ref_tile(_ffn_tile(x_ref_tile(), g_ref, wgu_bf, wd_bf, gf_ref, a_scr, final_norm,
                             before_chunk=_ffn_weight_streamer(wgu_hbm, wd_hbm, scratch)))

    @pl.when(jnp.logical_not(first_step))
    def _weights_resident():
        o_ref_tile(_ffn_tile(x_ref_tile(), g_ref, wgu_bf, wd_bf, gf_ref, a_scr, final_norm))


def _ffn_kernel(x_ref, g_ref, wgu_hbm, wd_hbm, gf_ref, o_ref, a_scr, *weight_scratch,
                layer, final_norm):
    def write(y):
        o_ref[...] = y
    _ffn_tile_streaming(lambda: x_ref[...], write, pl.program_id(0) == 0,
                        wgu_hbm.at[layer], wd_hbm.at[layer], weight_scratch,
                        g_ref, gf_ref, a_scr, final_norm)


def _ffn_vmem_estimate(tm):
    return (4 * tm * D_MODEL * 4
            + 3 * D_MODEL * D_FF * 2
            + 2 * 3 * D_MODEL * FF_CHUNK * 4
            + tm * D_FF * 2
            + 6 * tm * D_MODEL * 4)


def _ffn(x2d, g, wgu, wd, g_final, *, layer, final_norm):
    n_tok = x2d.shape[0]
    tm = FFN_TILE
    est = _ffn_vmem_estimate(tm)
    return pl.pallas_call(
        functools.partial(_ffn_kernel, layer=layer, final_norm=final_norm),
        out_shape=jax.ShapeDtypeStruct((n_tok, D_MODEL), F32),
        grid=(n_tok // tm,),
        in_specs=[
            pl.BlockSpec((tm, D_MODEL), lambda i: (i, 0)),
            _resident((1, D_MODEL), layer),
            pl.BlockSpec(memory_space=pl.ANY),
            pl.BlockSpec(memory_space=pl.ANY),
            _resident((1, D_MODEL)),
        ],
        out_specs=pl.BlockSpec((tm, D_MODEL), lambda i: (i, 0)),
        scratch_shapes=[pltpu.VMEM((tm, D_FF), BF16)] + _ffn_weight_scratch(),
        compiler_params=pltpu.CompilerParams(
            dimension_semantics=("arbitrary",), vmem_limit_bytes=_vmem_limit(est)),
        name="ffn_final" if final_norm else "ffn",
    )(x2d, g, wgu, wd, g_final)


def _class_rows(c, tm):
    return pl.ds(c, tm // ATTN_CLASSES, stride=ATTN_CLASSES)


def _lane_block(j):
    return slice(j * V7X_LANES, (j + 1) * V7X_LANES)


def _to_class_major(tile, scr, tm):
    n_blocks = tile.shape[1] // V7X_LANES
    for j in range(n_blocks):
        scr[j] = tile[:, _lane_block(j)]
    return jnp.concatenate(
        [jnp.concatenate([scr[j, _class_rows(c, tm), :] for j in range(n_blocks)], axis=1)
         for c in range(ATTN_CLASSES)], axis=0)


def _from_class_major(tile, scr, tm):
    n_blocks = tile.shape[1] // V7X_LANES
    per_class = tm // ATTN_CLASSES
    for j in range(n_blocks):
        for c in range(ATTN_CLASSES):
            scr[j, _class_rows(c, tm), :] = tile[c * per_class:(c + 1) * per_class, _lane_block(j)]
    return jnp.concatenate([scr[j] for j in range(n_blocks)], axis=1)


def _norm_proj_kernel(x_ref, g_ref, w_hbm, o_ref, scr, w_bf, w_stage, w_sems, *, tm, w_layer):
    @pl.when((pl.program_id(0) == 0) & (pl.program_id(1) == 0))
    def _load_weights():
        _load_weight_bf16(w_hbm.at[w_layer], w_bf, w_stage, w_sems)

    per_class = tm // ATTN_CLASSES
    h = _rms_norm(_to_class_major(x_ref[0], scr, tm), g_ref[...]).astype(BF16)
    y = jnp.dot(h, w_bf[...], preferred_element_type=F32)
    for c in range(ATTN_CLASSES):
        o_ref[c, 0] = y[c * per_class:(c + 1) * per_class]


def _norm_proj(x, g, w, *, layer, w_layer):
    bsz, seq, _ = x.shape
    n_out = w.shape[-1]
    tm = TOKEN_TILE
    per_class = tm // ATTN_CLASSES
    est = (2 * tm * D_MODEL * 4 + 2 * tm * n_out * 4 + D_MODEL * n_out * 2 + 2 * tm * n_out * 4
           + 2 * WEIGHT_CHUNK_ROWS * n_out * 4)
    return pl.pallas_call(
        functools.partial(_norm_proj_kernel, tm=tm, w_layer=w_layer),
        out_shape=jax.ShapeDtypeStruct((ATTN_CLASSES, bsz, seq // ATTN_CLASSES, n_out), F32),
        grid=(bsz, seq // tm),
        in_specs=[
            pl.BlockSpec((1, tm, D_MODEL), lambda b, i: (b, i, 0)),
            _resident((1, D_MODEL), layer),
            pl.BlockSpec(memory_space=pl.ANY),
        ],
        out_specs=pl.BlockSpec((ATTN_CLASSES, 1, per_class, n_out), lambda b, i: (0, b, i, 0)),
        scratch_shapes=[pltpu.VMEM((D_MODEL // V7X_LANES, tm, V7X_LANES), F32)]
        + _weight_scratch(w.shape[1:]),
        compiler_params=pltpu.CompilerParams(
            dimension_semantics=("arbitrary",) * 2, vmem_limit_bytes=_vmem_limit(est)),
        name="norm_proj",
    )(x, g, w)


def _attn_out_ffn_kernel(x_ref, y_ref, wo_hbm, g_ref, wgu_hbm, wd_hbm, gf_ref, o_ref,
                         a_scr, cm_scr, wo_bf, wo_stage, wo_sems, *weight_scratch,
                         tm, layer, w_layer, final_norm):
    first_step = (pl.program_id(0) == 0) & (pl.program_id(1) == 0)

    @pl.when(first_step)
    def _load_output_projection():
        _load_weight_bf16(wo_hbm.at[w_layer], wo_bf, wo_stage, wo_sems)

    def read():
        y = jnp.concatenate([y_ref[c, 0] for c in range(ATTN_CLASSES)], axis=0).astype(BF16)
        r = jnp.dot(y, wo_bf[...], preferred_element_type=F32)
        return x_ref[0] + _from_class_major(r, cm_scr, tm)

    def write(y):
        o_ref[0] = y
    _ffn_tile_streaming(read, write, first_step, wgu_hbm.at[layer], wd_hbm.at[layer],
                        weight_scratch, g_ref, gf_ref, a_scr, final_norm)


def _attn_out_ffn(x, y, w_o, g, wgu, wd, g_final, *, layer, w_layer, final_norm):
    bsz, seq, _ = x.shape
    tm = TOKEN_TILE
    per_class = tm // ATTN_CLASSES
    est = (_ffn_vmem_estimate(tm) + ATTN_WIDTH * D_MODEL * 2 + 2 * WEIGHT_CHUNK_ROWS * D_MODEL * 4
           + 2 * tm * ATTN_WIDTH * 4
           + 3 * tm * D_MODEL * 4)
    x_tile = pl.BlockSpec((1, tm, D_MODEL), lambda b, i: (b, i, 0))
    return pl.pallas_call(
        functools.partial(_attn_out_ffn_kernel, tm=tm, layer=layer, w_layer=w_layer,
                          final_norm=final_norm),
        out_shape=jax.ShapeDtypeStruct(x.shape, F32),
        grid=(bsz, seq // tm),
        in_specs=[
            x_tile,
            pl.BlockSpec((ATTN_CLASSES, 1, per_class, y.shape[-1]), lambda b, i: (0, b, i, 0)),
            pl.BlockSpec(memory_space=pl.ANY),
            _resident((1, D_MODEL), layer),
            pl.BlockSpec(memory_space=pl.ANY),
            pl.BlockSpec(memory_space=pl.ANY),
            _resident((1, D_MODEL)),
        ],
        out_specs=x_tile,
        scratch_shapes=[pltpu.VMEM((tm, D_FF), BF16),
                        pltpu.VMEM((D_MODEL // V7X_LANES, tm, V7X_LANES), F32)]
        + _weight_scratch(w_o.shape[1:]) + _ffn_weight_scratch(),
        compiler_params=pltpu.CompilerParams(
            dimension_semantics=("arbitrary",) * 2, vmem_limit_bytes=_vmem_limit(est)),
        name="attn_out_ffn_final" if final_norm else "attn_out_ffn",
    )(x, y, w_o, g, wgu, wd, g_final)


def _conv_mixer_kernel(x_ref, g_ref, w_in_hbm, ka_ref, kb_ref, bb_ref, lng_ref, lnb_ref,
                       w_out_hbm, o_ref, cx_scr, u_scr, cv_scr,
                       w_in_ref, w_in_stage, w_in_sems, w_out_ref, w_out_stage, w_out_sems,
                       *, tm, w_layer):
    ha, hb = CONV_A_HALO, CONV_B_HALO

    @pl.when((pl.program_id(0) == 0) & (pl.program_id(1) == 0))
    def _load_weights():
        _load_weight_bf16(w_in_hbm.at[w_layer], w_in_ref, w_in_stage, w_in_sems)
        _load_weight_bf16(w_out_hbm.at[w_layer], w_out_ref, w_out_stage, w_out_sems)

    @pl.when(pl.program_id(1) == 0)
    def _start_of_sequence():
        cx_scr[0:ha, :] = jnp.zeros((ha, SC_WIDTH), F32)
        u_scr[0:hb, :] = jnp.zeros((hb, CM_WIDTH), F32)

    x = x_ref[0]
    h = _rms_norm(x, g_ref[...]).astype(BF16)

    def proj(lo, width):
        return jnp.dot(h, w_in_ref[:, lo:lo + width], preferred_element_type=F32)

    u_scr[hb:hb + tm, :] = proj(3 * SC_WIDTH, CM_WIDTH) * _sigmoid(
        proj(3 * SC_WIDTH + CM_WIDTH, CM_WIDTH))

    first_off = hb - (CM_KERNEL - 1)

    def conv_b_rows(r0):
        for j in range(CM_WIDTH // V7X_LANES):
            lanes = slice(j * V7X_LANES, (j + 1) * V7X_LANES)
            n_win = CONV_ROWS + hb
            window = u_scr[pl.ds(r0, n_win), lanes]
            acc = jnp.zeros((CONV_ROWS, V7X_LANES), F32)
            for sub in range(V7X_SUBLANES):
                offs = [o for o in range(first_off, hb + 1) if o % V7X_SUBLANES == sub]
                shifted = pltpu.roll(window, n_win - sub, axis=0) if sub else window
                for o in offs:
                    k = o - first_off
                    acc = acc + kb_ref[k:k + 1, lanes] * shifted[o - sub:o - sub + CONV_ROWS, :]
            cv_scr[pl.ds(r0, CONV_ROWS), lanes] = acc

    a_b = []

    def mixer_a_cols(lo):
        a_b.append(proj(lo, V7X_MXU_COLS))
        cx_scr[ha:ha + tm, lo:lo + V7X_MXU_COLS] = (
            proj(SC_WIDTH + lo, V7X_MXU_COLS) * proj(2 * SC_WIDTH + lo, V7X_MXU_COLS))

    def mixer_a_output():
        conv_a = ka_ref[0:1, :] * cx_scr[ha - 2:ha - 2 + tm, :]
        conv_a = conv_a + ka_ref[1:2, :] * cx_scr[ha - 1:ha - 1 + tm, :]
        conv_a = conv_a + ka_ref[2:3, :] * cx_scr[ha:ha + tm, :]
        y_a = (jnp.concatenate(a_b, axis=1) * conv_a).astype(BF16)
        return jnp.dot(y_a, w_out_ref[0:SC_WIDTH, :], preferred_element_type=F32)

    out_a = None
    for idx, r0 in enumerate(range(0, tm, CONV_ROWS)):
        conv_b_rows(r0)
        if idx < SC_WIDTH // V7X_MXU_COLS:
            mixer_a_cols(idx * V7X_MXU_COLS)
        elif out_a is None:
            out_a = mixer_a_output()

    u = cv_scr[...] + bb_ref[...]
    mu = jnp.mean(u, axis=-1, keepdims=True)
    uc = u - mu
    un = uc * lax.rsqrt(jnp.mean(uc * uc, axis=-1, keepdims=True) + LN_EPS)
    un = un * lng_ref[...] + lnb_ref[...]
    y_b = (un * _sigmoid(un)).astype(BF16)
    o_ref[0] = x + out_a + jnp.dot(y_b, w_out_ref[SC_WIDTH:SC_WIDTH + CM_WIDTH, :],
                                   preferred_element_type=F32)

    cx_scr[0:ha, :] = cx_scr[tm:tm + ha, :]
    u_scr[0:hb, :] = u_scr[tm:tm + hb, :]


def _conv_mixer(x, g, w_in, ka, kb, bb, lng, lnb, w_out, *, layer, w_layer):
    bsz, seq, _ = x.shape
    tm = TOKEN_TILE
    est = (4 * tm * D_MODEL * 4 + (D_MODEL * IN_AB + D_MODEL * D_MODEL) * 2
           + 2 * WEIGHT_CHUNK_ROWS * (IN_AB + D_MODEL) * 4
           + (3 * tm + 64) * SC_WIDTH * 4 + tm * D_MODEL * 2
           + 8 * tm * D_MODEL * 4)
    tile = pl.BlockSpec((1, tm, D_MODEL), lambda b, t: (b, t, 0))
    return pl.pallas_call(
        functools.partial(_conv_mixer_kernel, tm=tm, w_layer=w_layer),
        out_shape=jax.ShapeDtypeStruct(x.shape, F32),
        grid=(bsz, seq // tm),
        in_specs=[
            tile,
            _resident((1, D_MODEL), layer),
            pl.BlockSpec(memory_space=pl.ANY),
            _resident(ka.shape[1:], w_layer),
            _resident(kb.shape[1:], w_layer),
            _resident((1, CM_WIDTH), w_layer),
            _resident((1, CM_WIDTH), w_layer),
            _resident((1, CM_WIDTH), w_layer),
            pl.BlockSpec(memory_space=pl.ANY),
        ],
        out_specs=tile,
        scratch_shapes=[
            pltpu.VMEM((CONV_A_HALO + tm, SC_WIDTH), F32),
            pltpu.VMEM((CONV_B_HALO + tm, CM_WIDTH), F32),
            pltpu.VMEM((tm, CM_WIDTH), F32),
        ] + _weight_scratch(w_in.shape[1:]) + _weight_scratch(w_out.shape[1:]),
        compiler_params=pltpu.CompilerParams(
            dimension_semantics=("arbitrary", "arbitrary"),
            vmem_limit_bytes=_vmem_limit(est)),
        name="conv_mixer",
    )(x, g, w_in, ka, kb, bb, lng, lnb, w_out)


def _attention_kernel(slopes_ref, q_ref, k_ref, v_ref, o_ref, m_scr, l_scr, *, seq):
    blk, grp, ncls = ATTN_BLOCK, ATTN_GROUP, ATTN_CLASSES
    sub = blk // ncls
    pair = pl.program_id(1)
    slope = [slopes_ref[HEADS_PER_STEP * pair + h] for h in range(HEADS_PER_STEP)]
    first_head = lax.broadcasted_iota(jnp.int32, (1, V7X_LANES), 1) < HEAD_DIM
    head_lanes = [first_head, jnp.logical_not(first_head)]
    nt_dims = (((1,), (1,)), ((), ()))
    q_scale = LOG2_E / math.sqrt(HEAD_DIM)

    q_cls = [q_ref.at[c, 0] for c in range(ncls)]
    k_cls = [k_ref.at[c, 0] for c in range(ncls)]
    v_cls = [v_ref.at[c, 0] for c in range(ncls)]
    o_cls = [o_ref.at[c, 0] for c in range(ncls)]
    m_cls = [m_scr.at[c] for c in range(ncls)]
    l_cls = [l_scr.at[c] for c in range(ncls)]

    def class_rows(c, j):
        def load(refs, first, n):
            start = pl.multiple_of((j * grp + first) * blk, blk)
            return refs[c][pl.ds(start, n * blk), :]

        def store(refs, value):
            refs[c][pl.ds(pl.multiple_of(j * grp * blk, blk), grp * blk), :] = value
        return load, store

    def subclass_rows(c, a, step):
        def load(refs, first, n):
            assert first == 0 and n == grp
            return refs[c][pl.ds(a, grp * blk, stride=step), :]

        def store(refs, value):
            refs[c][pl.ds(a, grp * blk, stride=step), :] = value
        return load, store

    def natural_rows(j):
        def load(refs, first, n):
            start = pl.multiple_of((j * grp + first) * sub, sub)
            per_class = [refs[c][pl.ds(start, n * sub), :] for c in range(ncls)]
            return jnp.concatenate(
                [per_class[c][b * sub:(b + 1) * sub] for b in range(n) for c in range(ncls)], axis=0)

        def store(refs, value):
            start = pl.multiple_of(j * grp * sub, sub)
            for c in range(ncls):
                refs[c][pl.ds(start, grp * sub), :] = jnp.concatenate(
                    [value[b * blk + c * sub:b * blk + (c + 1) * sub] for b in range(grp)], axis=0)
        return load, store

    def run_group(access, first_group, bias_one, bias_two, first_branch, last_branch):
        load, store = access
        q = load(q_cls, 0, grp) * q_scale
        k_first, k_blocks = (0, grp) if first_group else (-1, grp + 1)
        kb = load(k_cls, k_first, k_blocks).astype(BF16)
        vb = load(v_cls, k_first, k_blocks).astype(BF16)
        vb = jnp.concatenate([vb, jnp.ones(vb.shape, BF16)], axis=1)
        qh = [jnp.where(head_lanes[h], q, 0.0).astype(BF16) for h in range(HEADS_PER_STEP)]

        ms, ls, pvs = [], [], []
        for u in range(grp):
            if first_group and u == 0:
                keys, bias = slice(0, blk), bias_one
            else:
                lo = u - 1 if first_group else u
                keys, bias = slice(lo * blk, (lo + 2) * blk), bias_two
            rows = slice(u * blk, (u + 1) * blk)
            lhs = jnp.concatenate([qh[h][rows] for h in range(HEADS_PER_STEP)], axis=0)
            s = lax.dot_general(lhs, kb[keys], nt_dims, preferred_element_type=F32) + bias
            m = jnp.max(s, axis=-1, keepdims=True)
            p = jnp.exp2(s - m).astype(BF16)
            r = jnp.dot(p, vb[keys], preferred_element_type=F32)
            mb = jnp.broadcast_to(m, (HEADS_PER_STEP * blk, V7X_LANES))
            ms.append(jnp.where(first_head, mb[:blk], mb[blk:]))
            pvs.append(jnp.where(first_head, r[:blk, :V7X_LANES], r[blk:, :V7X_LANES]))
            ls.append(jnp.where(first_head, r[:blk, V7X_LANES:], r[blk:, V7X_LANES:]))
        m_new = jnp.concatenate(ms, axis=0)
        l_new = jnp.concatenate(ls, axis=0)
        acc = jnp.concatenate(pvs, axis=0)

        if not first_branch:
            m_old, l_old, acc_old = load(m_cls, 0, grp), load(l_cls, 0, grp), load(o_cls, 0, grp)
            top = jnp.maximum(m_old, m_new)
            w_old = jnp.exp2(m_old - top)
            w_new = jnp.exp2(m_new - top)
            l_new = w_old * l_old + w_new * l_new
            acc = w_old * acc_old + w_new * acc
            m_new = top
        if last_branch:
            store(o_cls, acc / l_new)
        else:
            store(m_cls, m_new)
            store(l_cls, l_new)
            store(o_cls, acc)

    def bias_tables(dil, natural):
        def position(idx):
            if not natural:
                return idx
            inside = idx & (blk - 1)
            return (idx - inside) + ncls * (inside & (sub - 1)) + (inside >> int(math.log2(sub)))
        iq = position(lax.broadcasted_iota(jnp.int32, (blk, 2 * blk), 0))
        ik = position(lax.broadcasted_iota(jnp.int32, (blk, 2 * blk), 1))
        rel = blk + iq - ik
        valid = (rel >= 0) & (rel <= ATTN_BLOCK)
        dist = (dil * rel).astype(F32)
        two = jnp.concatenate(
            [jnp.where(valid, (-LOG2_E * slope[h]) * dist, NEG_INF) for h in range(HEADS_PER_STEP)],
            axis=0)
        return two[:, blk:], two

    rows_per_class = seq // ncls
    for branch, (window, dil) in enumerate(DILATED_BRANCHES):
        assert window // dil == ATTN_BLOCK
        first_branch, last_branch = branch == 0, branch == len(DILATED_BRANCHES) - 1
        bias_one, bias_two = bias_tables(dil, natural=(dil == 1))
        run = functools.partial(run_group, bias_one=bias_one, bias_two=bias_two,
                                first_branch=first_branch, last_branch=last_branch)

        def sweep(make_access, n_groups, run=run):
            run(make_access(0), True)
            if n_groups > 1:
                def later(j, carry):
                    run(make_access(j), False)
                    return carry
                lax.fori_loop(1, n_groups, later, 0, unroll=ATTN_UNROLL[dil])

        if dil == 1:
            sweep(natural_rows, seq // (grp * blk))
        elif dil == ncls:
            for c in range(ncls):
                sweep(functools.partial(class_rows, c), rows_per_class // (grp * blk))
        else:
            step = dil // ncls
            assert rows_per_class // step == grp * blk
            for c in range(ncls):
                def subclass(a, carry, c=c, step=step, run=run):
                    run(subclass_rows(c, a, step), True)
                    return carry
                lax.fori_loop(0, step, subclass, 0, unroll=ATTN_UNROLL[dil])


def _attention(slopes, qkv):
    ncls, bsz, rows, _ = qkv.shape
    n_pairs = N_HEADS // HEADS_PER_STEP
    block_bytes = ncls * rows * V7X_LANES * 4
    est = (2 * 4 * block_bytes
           + 2 * block_bytes
           + 6 * 1024 * 1024)

    def cols(offset):
        return pl.BlockSpec((ncls, 1, rows, V7X_LANES), lambda b, p: (0, b, 0, offset + p))

    return pl.pallas_call(
        functools.partial(_attention_kernel, seq=ncls * rows),
        out_shape=jax.ShapeDtypeStruct((ncls, bsz, rows, ATTN_WIDTH), F32),
        grid=(bsz, n_pairs),
        in_specs=[
            pl.BlockSpec(memory_space=pltpu.SMEM),
            cols(0), cols(n_pairs), cols(2 * n_pairs),
        ],
        out_specs=cols(0),
        scratch_shapes=[pltpu.VMEM((ncls, rows, V7X_LANES), F32),
                        pltpu.VMEM((ncls, rows, V7X_LANES), F32)],
        compiler_params=pltpu.CompilerParams(
            dimension_semantics=("arbitrary", "arbitrary"),
            vmem_limit_bytes=_vmem_limit(est)),
        name="dilated_attention",
    )(slopes, qkv, qkv, qkv)


def _alibi_slopes():
    return np.array([2.0 ** (-8.0 * (i + 1) / N_HEADS) for i in range(N_HEADS)], dtype=np.float32)


def kernel(x, ffn1_norm, ffn1_w_gate_up, ffn1_w_down, mix_norm, ffn2_norm, ffn2_w_gate_up,
           ffn2_w_down, conv_w_in, conv_a_kernel, conv_b_kernel, conv_b_bias, conv_b_ln_gain,
           conv_b_ln_bias, conv_w_out, attn_w_qkv, attn_w_o, final_norm):
    bsz, seq, d = x.shape
    n_tok = bsz * seq
    rows = lambda v: v.reshape(v.shape[0], 1, v.shape[1])
    slopes = jnp.asarray(_alibi_slopes())
    g_final = final_norm.reshape(1, -1)
    ffn1_norm, mix_norm, ffn2_norm = rows(ffn1_norm), rows(mix_norm), rows(ffn2_norm)
    conv_b_bias, conv_b_ln_gain, conv_b_ln_bias = (
        rows(conv_b_bias), rows(conv_b_ln_gain), rows(conv_b_ln_bias))
    ffn1_wgu, ffn1_wd = ffn1_w_gate_up, ffn1_w_down
    ffn2_wgu, ffn2_wd = ffn2_w_gate_up, ffn2_w_down

    for layer in range(DEPTH):
        x = _ffn(x.reshape(n_tok, d), ffn1_norm, ffn1_wgu, ffn1_wd, g_final,
                 layer=layer, final_norm=False).reshape(bsz, seq, d)
        i = layer // 2
        last = layer == DEPTH - 1
        if layer % 2 == 0:
            x = _conv_mixer(x, mix_norm, conv_w_in, conv_a_kernel, conv_b_kernel, conv_b_bias,
                            conv_b_ln_gain, conv_b_ln_bias, conv_w_out, layer=layer, w_layer=i)
            x = _ffn(x.reshape(n_tok, d), ffn2_norm, ffn2_wgu, ffn2_wd, g_final,
                     layer=layer, final_norm=last).reshape(bsz, seq, d)
        else:
            qkv = _norm_proj(x, mix_norm, attn_w_qkv, layer=layer, w_layer=i)
            o = _attention(slopes, qkv)
            x = _attn_out_ffn(x, o, attn_w_o, ffn2_norm, ffn2_wgu, ffn2_wd, g_final,
                              layer=layer, w_layer=i, final_norm=last)
    return x
```

```python
import functools
import math

import numpy as np
import jax
import jax.numpy as jnp
from jax import lax
from jax.experimental import pallas as pl
from jax.experimental.pallas import tpu as pltpu

D_MODEL = 1024
DEPTH = 4
D_FF = 2816
RMS_EPS = 1e-6
LN_EPS = 1e-5
SC_WIDTH = 512
SC_KERNEL = 3
CM_WIDTH = 512
CM_KERNEL = 31
IN_AB = 3 * SC_WIDTH + 2 * CM_WIDTH
N_HEADS = 16
HEAD_DIM = 64
ATTN_WIDTH = N_HEADS * HEAD_DIM
DILATED_BRANCHES = ((128, 1), (512, 4), (2048, 16))
ATTN_BLOCK = 128
NEG_INF = -1e30

V7X_LANES = 128
V7X_SUBLANES = 8
V7X_MXU_COLS = 256
V7X_VMEM_BYTES = 64 * 1024 * 1024

F32 = jnp.float32
BF16 = jnp.bfloat16

TOKEN_TILE = 512
FFN_TILE = 1024
FF_CHUNK = V7X_MXU_COLS
FFN_STAGE_SLOTS = 3
CONV_ROWS = 64
CONV_A_HALO = V7X_SUBLANES
CONV_B_HALO = 32
HEADS_PER_STEP = V7X_LANES // HEAD_DIM
ATTN_GROUP = 4
ATTN_CLASSES = 4
ATTN_UNROLL = {1: 5, 4: 3, 16: 2}
LOG2_E = math.log2(math.e)


def _vmem_limit(estimate_bytes):
    return int(min(V7X_VMEM_BYTES - 8 * 1024 * 1024, estimate_bytes))


def _rms_norm(x, g):
    return x * lax.rsqrt(jnp.mean(x * x, axis=-1, keepdims=True) + RMS_EPS) * g


def _sigmoid(x):
    return 1.0 / (1.0 + jnp.exp(-x))


def _resident(shape, layer=None):
    zeros = (0,) * len(shape)
    if layer is None:
        return pl.BlockSpec(shape, lambda *_: zeros, pipeline_mode=pl.Buffered(1))
    return pl.BlockSpec((None,) + tuple(shape), lambda *_: (layer,) + zeros,
                        pipeline_mode=pl.Buffered(1))


def _ffn_tile(x, g_ref, wgu_ref, wd_ref, gf_ref, a_scr, final_norm, before_chunk=None):
    h = _rms_norm(x, g_ref[...]).astype(BF16)
    for c in range(D_FF // FF_CHUNK):
        lo = c * FF_CHUNK
        if before_chunk is not None:
            before_chunk(c)
        gate = jnp.dot(h, wgu_ref[:, lo:lo + FF_CHUNK], preferred_element_type=F32)
        up = jnp.dot(h, wgu_ref[:, D_FF + lo:D_FF + lo + FF_CHUNK], preferred_element_type=F32)
        a_scr[:, lo:lo + FF_CHUNK] = (gate * _sigmoid(gate) * up).astype(BF16)
    y = x + 0.5 * jnp.dot(a_scr[...], wd_ref[...], preferred_element_type=F32)
    if final_norm:
        y = _rms_norm(y, gf_ref[...])
    return y


def _ffn_weight_scratch():
    stage = lambda shape: pltpu.VMEM((FFN_STAGE_SLOTS,) + shape, F32)
    return [pltpu.VMEM((D_MODEL, 2 * D_FF), BF16),
            pltpu.VMEM((D_FF, D_MODEL), BF16),
            stage((D_MODEL, FF_CHUNK)), stage((D_MODEL, FF_CHUNK)), stage((FF_CHUNK, D_MODEL)),
            pltpu.SemaphoreType.DMA((FFN_STAGE_SLOTS, 3))]


def _ffn_weight_streamer(wgu_hbm, wd_hbm, scratch):
    wgu_bf, wd_bf, stage_g, stage_u, stage_d, sems = scratch
    n_chunks = D_FF // FF_CHUNK

    def parts(c):
        slot, lo = c % FFN_STAGE_SLOTS, c * FF_CHUNK
        gate_cols, up_cols = pl.ds(lo, FF_CHUNK), pl.ds(D_FF + lo, FF_CHUNK)
        down_rows = pl.ds(lo, FF_CHUNK)
        return [
            (pltpu.make_async_copy(wgu_hbm.at[:, gate_cols], stage_g.at[slot], sems.at[slot, 0]),
             stage_g, wgu_bf.at[:, gate_cols]),
            (pltpu.make_async_copy(wgu_hbm.at[:, up_cols], stage_u.at[slot], sems.at[slot, 1]),
             stage_u, wgu_bf.at[:, up_cols]),
            (pltpu.make_async_copy(wd_hbm.at[down_rows, :], stage_d.at[slot], sems.at[slot, 2]),
             stage_d, wd_bf.at[down_rows, :]),
        ]

    def fetch(c):
        for copy, _, _ in parts(c):
            copy.start()

    ahead = FFN_STAGE_SLOTS - 1

    def before_chunk(c):
        if c == 0:
            for first in range(ahead):
                fetch(first)
        if c + ahead < n_chunks:
            fetch(c + ahead)
        for copy, stage, dst in parts(c):
            copy.wait()
            dst[...] = stage[c % FFN_STAGE_SLOTS].astype(BF16)

    return before_chunk


def _ffn_tile_streaming(x_ref_tile, o_ref_tile, first_step, wgu_hbm, wd_hbm, scratch,
                        g_ref, gf_ref, a_scr, final_norm):
    wgu_bf, wd_bf = scratch[0], scratch[1]

    @pl.when(first_step)
    def _with_weight_load():
        o_ref_tile(_ffn_tile(x_ref_tile(), g_ref, wgu_bf, wd_bf, gf_ref, a_scr, final_norm,
                             before_chunk=_ffn_weight_streamer(wgu_hbm, wd_hbm, scratch)))

    @pl.when(jnp.logical_not(first_step))
    def _weights_resident():
        o_ref_tile(_ffn_tile(x_ref_tile(), g_ref, wgu_bf, wd_bf, gf_ref, a_scr, final_norm))


def _ffn_kernel(x_ref, g_ref, wgu_hbm, wd_hbm, gf_ref, o_ref, a_scr, *weight_scratch,
                layer, final_norm):
    def write(y):
        o_ref[...] = y
    _ffn_tile_streaming(lambda: x_ref[...], write, pl.program_id(0) == 0,
                        wgu_hbm.at[layer], wd_hbm.at[layer], weight_scratch,
                        g_ref, gf_ref, a_scr, final_norm)


def _ffn_vmem_estimate(tm):
    return (4 * tm * D_MODEL * 4
            + 3 * D_MODEL * D_FF * 2
            + FFN_STAGE_SLOTS * 3 * D_MODEL * FF_CHUNK * 4
            + tm * D_FF * 2
            + 6 * tm * D_MODEL * 4)


def _ffn(x2d, g, wgu, wd, g_final, *, layer, final_norm):
    n_tok = x2d.shape[0]
    tm = FFN_TILE
    est = _ffn_vmem_estimate(tm)
    return pl.pallas_call(
        functools.partial(_ffn_kernel, layer=layer, final_norm=final_norm),
        out_shape=jax.ShapeDtypeStruct((n_tok, D_MODEL), F32),
        grid=(n_tok // tm,),
        in_specs=[
            pl.BlockSpec((tm, D_MODEL), lambda i: (i, 0)),
            _resident((1, D_MODEL), layer),
            pl.BlockSpec(memory_space=pl.ANY),
            pl.BlockSpec(memory_space=pl.ANY),
            _resident((1, D_MODEL)),
        ],
        out_specs=pl.BlockSpec((tm, D_MODEL), lambda i: (i, 0)),
        scratch_shapes=[pltpu.VMEM((tm, D_FF), BF16)] + _ffn_weight_scratch(),
        compiler_params=pltpu.CompilerParams(
            dimension_semantics=("arbitrary",), vmem_limit_bytes=_vmem_limit(est)),
        name="ffn_final" if final_norm else "ffn",
    )(x2d, g, wgu, wd, g_final)


def _class_rows(c, tm):
    return pl.ds(c, tm // ATTN_CLASSES, stride=ATTN_CLASSES)


def _lane_block(j):
    return slice(j * V7X_LANES, (j + 1) * V7X_LANES)


def _to_class_major(tile, scr, tm):
    n_blocks = tile.shape[1] // V7X_LANES
    for j in range(n_blocks):
        scr[j] = tile[:, _lane_block(j)]
    return jnp.concatenate(
        [jnp.concatenate([scr[j, _class_rows(c, tm), :] for j in range(n_blocks)], axis=1)
         for c in range(ATTN_CLASSES)], axis=0)


def _from_class_major(tile, scr, tm):
    n_blocks = tile.shape[1] // V7X_LANES
    per_class = tm // ATTN_CLASSES
    for j in range(n_blocks):
        for c in range(ATTN_CLASSES):
            scr[j, _class_rows(c, tm), :] = tile[c * per_class:(c + 1) * per_class, _lane_block(j)]
    return jnp.concatenate([scr[j] for j in range(n_blocks)], axis=1)


def _norm_proj_kernel(x_ref, g_ref, w_ref, o_ref, scr, *, tm):
    per_class = tm // ATTN_CLASSES
    h = _rms_norm(_to_class_major(x_ref[0], scr, tm), g_ref[...]).astype(BF16)
    y = jnp.dot(h, w_ref[...], preferred_element_type=F32)
    for c in range(ATTN_CLASSES):
        o_ref[c, 0] = y[c * per_class:(c + 1) * per_class]


def _norm_proj(x, g, w, *, layer, w_layer):
    bsz, seq, _ = x.shape
    n_out = w.shape[-1]
    tm = TOKEN_TILE
    per_class = tm // ATTN_CLASSES
    est = (2 * tm * D_MODEL * 4 + 2 * tm * n_out * 4 + D_MODEL * n_out * 2 + 2 * tm * n_out * 4)
    return pl.pallas_call(
        functools.partial(_norm_proj_kernel, tm=tm),
        out_shape=jax.ShapeDtypeStruct((ATTN_CLASSES, bsz, seq // ATTN_CLASSES, n_out), F32),
        grid=(bsz, seq // tm),
        in_specs=[
            pl.BlockSpec((1, tm, D_MODEL), lambda b, i: (b, i, 0)),
            _resident((1, D_MODEL), layer),
            _resident(w.shape[1:], w_layer),
        ],
        out_specs=pl.BlockSpec((ATTN_CLASSES, 1, per_class, n_out), lambda b, i: (0, b, i, 0)),
        scratch_shapes=[pltpu.VMEM((D_MODEL // V7X_LANES, tm, V7X_LANES), F32)],
        compiler_params=pltpu.CompilerParams(
            dimension_semantics=("arbitrary",) * 2, vmem_limit_bytes=_vmem_limit(est)),
        name="norm_proj",
    )(x, g, w)


def _attn_out_ffn_kernel(x_ref, y_ref, wo_ref, g_ref, wgu_hbm, wd_hbm, gf_ref, o_ref,
                         a_scr, cm_scr, *weight_scratch, tm, layer, final_norm):
    first_step = (pl.program_id(0) == 0) & (pl.program_id(1) == 0)

    def read():
        y = jnp.concatenate([y_ref[c, 0] for c in range(ATTN_CLASSES)], axis=0).astype(BF16)
        r = jnp.dot(y, wo_ref[...], preferred_element_type=F32)
        return x_ref[0] + _from_class_major(r, cm_scr, tm)

    def write(y):
        o_ref[0] = y
    _ffn_tile_streaming(read, write, first_step, wgu_hbm.at[layer], wd_hbm.at[layer],
                        weight_scratch, g_ref, gf_ref, a_scr, final_norm)


def _attn_out_ffn(x, y, w_o, g, wgu, wd, g_final, *, layer, w_layer, final_norm):
    bsz, seq, _ = x.shape
    tm = TOKEN_TILE
    per_class = tm // ATTN_CLASSES
    est = (_ffn_vmem_estimate(tm) + ATTN_WIDTH * D_MODEL * 2
           + 2 * tm * ATTN_WIDTH * 4
           + 3 * tm * D_MODEL * 4)
    x_tile = pl.BlockSpec((1, tm, D_MODEL), lambda b, i: (b, i, 0))
    return pl.pallas_call(
        functools.partial(_attn_out_ffn_kernel, tm=tm, layer=layer, final_norm=final_norm),
        out_shape=jax.ShapeDtypeStruct(x.shape, F32),
        grid=(bsz, seq // tm),
        in_specs=[
            x_tile,
            pl.BlockSpec((ATTN_CLASSES, 1, per_class, y.shape[-1]), lambda b, i: (0, b, i, 0)),
            _resident(w_o.shape[1:], w_layer),
            _resident((1, D_MODEL), layer),
            pl.BlockSpec(memory_space=pl.ANY),
            pl.BlockSpec(memory_space=pl.ANY),
            _resident((1, D_MODEL)),
        ],
        out_specs=x_tile,
        scratch_shapes=[pltpu.VMEM((tm, D_FF), BF16),
                        pltpu.VMEM((D_MODEL // V7X_LANES, tm, V7X_LANES), F32)]
        + _ffn_weight_scratch(),
        compiler_params=pltpu.CompilerParams(
            dimension_semantics=("arbitrary",) * 2, vmem_limit_bytes=_vmem_limit(est)),
        name="attn_out_ffn_final" if final_norm else "attn_out_ffn",
    )(x, y, w_o, g, wgu, wd, g_final)


def _conv_mixer_kernel(x_ref, g_ref, w_in_ref, ka_ref, kb_ref, bb_ref, lng_ref, lnb_ref,
                       w_out_ref, o_ref, cx_scr, u_scr, cv_scr, *, tm):
    ha, hb = CONV_A_HALO, CONV_B_HALO

    @pl.when(pl.program_id(1) == 0)
    def _start_of_sequence():
        cx_scr[0:ha, :] = jnp.zeros((ha, SC_WIDTH), F32)
        u_scr[0:hb, :] = jnp.zeros((hb, CM_WIDTH), F32)

    x = x_ref[0]
    h = _rms_norm(x, g_ref[...]).astype(BF16)

    def proj(lo, width):
        return jnp.dot(h, w_in_ref[:, lo:lo + width], preferred_element_type=F32)

    u_scr[hb:hb + tm, :] = proj(3 * SC_WIDTH, CM_WIDTH) * _sigmoid(
        proj(3 * SC_WIDTH + CM_WIDTH, CM_WIDTH))

    first_off = hb - (CM_KERNEL - 1)

    def conv_b_rows(r0):
        for j in range(CM_WIDTH // V7X_LANES):
            lanes = slice(j * V7X_LANES, (j + 1) * V7X_LANES)
            n_win = CONV_ROWS + hb
            window = u_scr[pl.ds(r0, n_win), lanes]
            acc = jnp.zeros((CONV_ROWS, V7X_LANES), F32)
            for sub in range(V7X_SUBLANES):
                offs = [o for o in range(first_off, hb + 1) if o % V7X_SUBLANES == sub]
                shifted = pltpu.roll(window, n_win - sub, axis=0) if sub else window
                for o in offs:
                    k = o - first_off
                    acc = acc + kb_ref[k:k + 1, lanes] * shifted[o - sub:o - sub + CONV_ROWS, :]
            cv_scr[pl.ds(r0, CONV_ROWS), lanes] = acc

    a_b = []

    def mixer_a_cols(lo):
        a_b.append(proj(lo, V7X_MXU_COLS))
        cx_scr[ha:ha + tm, lo:lo + V7X_MXU_COLS] = (
            proj(SC_WIDTH + lo, V7X_MXU_COLS) * proj(2 * SC_WIDTH + lo, V7X_MXU_COLS))

    def mixer_a_output():
        conv_a = ka_ref[0:1, :] * cx_scr[ha - 2:ha - 2 + tm, :]
        conv_a = conv_a + ka_ref[1:2, :] * cx_scr[ha - 1:ha - 1 + tm, :]
        conv_a = conv_a + ka_ref[2:3, :] * cx_scr[ha:ha + tm, :]
        y_a = (jnp.concatenate(a_b, axis=1) * conv_a).astype(BF16)
        return jnp.dot(y_a, w_out_ref[0:SC_WIDTH, :], preferred_element_type=F32)

    out_a = None
    for idx, r0 in enumerate(range(0, tm, CONV_ROWS)):
        conv_b_rows(r0)
        if idx < SC_WIDTH // V7X_MXU_COLS:
            mixer_a_cols(idx * V7X_MXU_COLS)
        elif out_a is None:
            out_a = mixer_a_output()

    u = cv_scr[...] + bb_ref[...]
    mu = jnp.mean(u, axis=-1, keepdims=True)
    uc = u - mu
    un = uc * lax.rsqrt(jnp.mean(uc * uc, axis=-1, keepdims=True) + LN_EPS)
    un = un * lng_ref[...] + lnb_ref[...]
    y_b = (un * _sigmoid(un)).astype(BF16)
    o_ref[0] = x + out_a + jnp.dot(y_b, w_out_ref[SC_WIDTH:SC_WIDTH + CM_WIDTH, :],
                                   preferred_element_type=F32)

    cx_scr[0:ha, :] = cx_scr[tm:tm + ha, :]
    u_scr[0:hb, :] = u_scr[tm:tm + hb, :]


def _conv_mixer(x, g, w_in, ka, kb, bb, lng, lnb, w_out, *, layer, w_layer):
    bsz, seq, _ = x.shape
    tm = TOKEN_TILE
    est = (4 * tm * D_MODEL * 4 + (D_MODEL * IN_AB + D_MODEL * D_MODEL) * 2
           + (3 * tm + 64) * SC_WIDTH * 4 + tm * D_MODEL * 2
           + 8 * tm * D_MODEL * 4)
    tile = pl.BlockSpec((1, tm, D_MODEL), lambda b, t: (b, t, 0))
    return pl.pallas_call(
        functools.partial(_conv_mixer_kernel, tm=tm),
        out_shape=jax.ShapeDtypeStruct(x.shape, F32),
        grid=(bsz, seq // tm),
        in_specs=[
            tile,
            _resident((1, D_MODEL), layer),
            _resident(w_in.shape[1:], w_layer),
            _resident(ka.shape[1:], w_layer),
            _resident(kb.shape[1:], w_layer),
            _resident((1, CM_WIDTH), w_layer),
            _resident((1, CM_WIDTH), w_layer),
            _resident((1, CM_WIDTH), w_layer),
            _resident(w_out.shape[1:], w_layer),
        ],
        out_specs=tile,
        scratch_shapes=[
            pltpu.VMEM((CONV_A_HALO + tm, SC_WIDTH), F32),
            pltpu.VMEM((CONV_B_HALO + tm, CM_WIDTH), F32),
            pltpu.VMEM((tm, CM_WIDTH), F32),
        ],
        compiler_params=pltpu.CompilerParams(
            dimension_semantics=("arbitrary", "arbitrary"),
            vmem_limit_bytes=_vmem_limit(est)),
        name="conv_mixer",
    )(x, g, w_in, ka, kb, bb, lng, lnb, w_out)


def _attention_kernel(slopes_ref, q_ref, k_ref, v_ref, o_ref, m_scr, l_scr, *, seq):
    blk, grp, ncls = ATTN_BLOCK, ATTN_GROUP, ATTN_CLASSES
    sub = blk // ncls
    pair = pl.program_id(1)
    slope = [slopes_ref[HEADS_PER_STEP * pair + h] for h in range(HEADS_PER_STEP)]
    first_head = lax.broadcasted_iota(jnp.int32, (1, V7X_LANES), 1) < HEAD_DIM
    head_lanes = [first_head, jnp.logical_not(first_head)]
    nt_dims = (((1,), (1,)), ((), ()))
    q_scale = LOG2_E / math.sqrt(HEAD_DIM)

    q_cls = [q_ref.at[c, 0] for c in range(ncls)]
    k_cls = [k_ref.at[c, 0] for c in range(ncls)]
    v_cls = [v_ref.at[c, 0] for c in range(ncls)]
    o_cls = [o_ref.at[c, 0] for c in range(ncls)]
    m_cls = [m_scr.at[c] for c in range(ncls)]
    l_cls = [l_scr.at[c] for c in range(ncls)]

    def class_rows(c, j):
        def load(refs, first, n):
            start = pl.multiple_of((j * grp + first) * blk, blk)
            return refs[c][pl.ds(start, n * blk), :]

        def store(refs, value):
            refs[c][pl.ds(pl.multiple_of(j * grp * blk, blk), grp * blk), :] = value
        return load, store

    def subclass_rows(c, a, step):
        def load(refs, first, n):
            assert first == 0 and n == grp
            return refs[c][pl.ds(a, grp * blk, stride=step), :]

        def store(refs, value):
            refs[c][pl.ds(a, grp * blk, stride=step), :] = value
        return load, store

    def natural_rows(j):
        def load(refs, first, n):
            start = pl.multiple_of((j * grp + first) * sub, sub)
            per_class = [refs[c][pl.ds(start, n * sub), :] for c in range(ncls)]
            return jnp.concatenate(
                [per_class[c][b * sub:(b + 1) * sub] for b in range(n) for c in range(ncls)], axis=0)

        def store(refs, value):
            start = pl.multiple_of(j * grp * sub, sub)
            for c in range(ncls):
                refs[c][pl.ds(start, grp * sub), :] = jnp.concatenate(
                    [value[b * blk + c * sub:b * blk + (c + 1) * sub] for b in range(grp)], axis=0)
        return load, store

    def run_group(access, first_group, bias_one, bias_two, first_branch, last_branch):
        load, store = access
        q = load(q_cls, 0, grp) * q_scale
        k_first, k_blocks = (0, grp) if first_group else (-1, grp + 1)
        kb = load(k_cls, k_first, k_blocks).astype(BF16)
        vb = load(v_cls, k_first, k_blocks).astype(BF16)
        vb = jnp.concatenate([vb, jnp.ones(vb.shape, BF16)], axis=1)
        qh = [jnp.where(head_lanes[h], q, 0.0).astype(BF16) for h in range(HEADS_PER_STEP)]

        ms, ls, pvs = [], [], []
        for u in range(grp):
            if first_group and u == 0:
                keys, bias = slice(0, blk), bias_one
            else:
                lo = u - 1 if first_group else u
                keys, bias = slice(lo * blk, (lo + 2) * blk), bias_two
            rows = slice(u * blk, (u + 1) * blk)
            lhs = jnp.concatenate([qh[h][rows] for h in range(HEADS_PER_STEP)], axis=0)
            s = lax.dot_general(lhs, kb[keys], nt_dims, preferred_element_type=F32) + bias
            m = jnp.max(s, axis=-1, keepdims=True)
            p = jnp.exp2(s - m).astype(BF16)
            r = jnp.dot(p, vb[keys], preferred_element_type=F32)
            mb = jnp.broadcast_to(m, (HEADS_PER_STEP * blk, V7X_LANES))
            ms.append(jnp.where(first_head, mb[:blk], mb[blk:]))
            pvs.append(jnp.where(first_head, r[:blk, :V7X_LANES], r[blk:, :V7X_LANES]))
            ls.append(jnp.where(first_head, r[:blk, V7X_LANES:], r[blk:, V7X_LANES:]))
        m_new = jnp.concatenate(ms, axis=0)
        l_new = jnp.concatenate(ls, axis=0)
        acc = jnp.concatenate(pvs, axis=0)

        if not first_branch:
            m_old, l_old, acc_old = load(m_cls, 0, grp), load(l_cls, 0, grp), load(o_cls, 0, grp)
            top = jnp.maximum(m_old, m_new)
            w_old = jnp.exp2(m_old - top)
            w_new = jnp.exp2(m_new - top)
            l_new = w_old * l_old + w_new * l_new
            acc = w_old * acc_old + w_new * acc
            m_new = top
        if last_branch:
            store(o_cls, acc / l_new)
        else:
            store(m_cls, m_new)
            store(l_cls, l_new)
            store(o_cls, acc)

    def bias_tables(dil, natural):
        def position(idx):
            if not natural:
                return idx
            inside = idx & (blk - 1)
            return (idx - inside) + ncls * (inside & (sub - 1)) + (inside >> int(math.log2(sub)))
        iq = position(lax.broadcasted_iota(jnp.int32, (blk, 2 * blk), 0))
        ik = position(lax.broadcasted_iota(jnp.int32, (blk, 2 * blk), 1))
        rel = blk + iq - ik
        valid = (rel >= 0) & (rel <= ATTN_BLOCK)
        dist = (dil * rel).astype(F32)
        two = jnp.concatenate(
            [jnp.where(valid, (-LOG2_E * slope[h]) * dist, NEG_INF) for h in range(HEADS_PER_STEP)],
            axis=0)
        return two[:, blk:], two

    rows_per_class = seq // ncls
    for branch, (window, dil) in enumerate(DILATED_BRANCHES):
        assert window // dil == ATTN_BLOCK
        first_branch, last_branch = branch == 0, branch == len(DILATED_BRANCHES) - 1
        bias_one, bias_two = bias_tables(dil, natural=(dil == 1))
        run = functools.partial(run_group, bias_one=bias_one, bias_two=bias_two,
                                first_branch=first_branch, last_branch=last_branch)

        def sweep(make_access, n_groups, run=run):
            run(make_access(0), True)
            if n_groups > 1:
                def later(j, carry):
                    run(make_access(j), False)
                    return carry
                lax.fori_loop(1, n_groups, later, 0, unroll=ATTN_UNROLL[dil])

        if dil == 1:
            sweep(natural_rows, seq // (grp * blk))
        elif dil == ncls:
            for c in range(ncls):
                sweep(functools.partial(class_rows, c), rows_per_class // (grp * blk))
        else:
            step = dil // ncls
            assert rows_per_class // step == grp * blk
            for c in range(ncls):
                def subclass(a, carry, c=c, step=step, run=run):
                    run(subclass_rows(c, a, step), True)
                    return carry
                lax.fori_loop(0, step, subclass, 0, unroll=ATTN_UNROLL[dil])


def _attention(slopes, qkv):
    ncls, bsz, rows, _ = qkv.shape
    n_pairs = N_HEADS // HEADS_PER_STEP
    block_bytes = ncls * rows * V7X_LANES * 4
    est = (2 * 4 * block_bytes
           + 2 * block_bytes
           + 6 * 1024 * 1024)

    def cols(offset):
        return pl.BlockSpec((ncls, 1, rows, V7X_LANES), lambda b, p: (0, b, 0, offset + p))

    return pl.pallas_call(
        functools.partial(_attention_kernel, seq=ncls * rows),
        out_shape=jax.ShapeDtypeStruct((ncls, bsz, rows, ATTN_WIDTH), F32),
        grid=(bsz, n_pairs),
        in_specs=[
            pl.BlockSpec(memory_space=pltpu.SMEM),
            cols(0), cols(n_pairs), cols(2 * n_pairs),
        ],
        out_specs=cols(0),
        scratch_shapes=[pltpu.VMEM((ncls, rows, V7X_LANES), F32),
                        pltpu.VMEM((ncls, rows, V7X_LANES), F32)],
        compiler_params=pltpu.CompilerParams(
            dimension_semantics=("arbitrary", "arbitrary"),
            vmem_limit_bytes=_vmem_limit(est)),
        name="dilated_attention",
    )(slopes, qkv, qkv, qkv)


def _alibi_slopes():
    return np.array([2.0 ** (-8.0 * (i + 1) / N_HEADS) for i in range(N_HEADS)], dtype=np.float32)


def kernel(x, ffn1_norm, ffn1_w_gate_up, ffn1_w_down, mix_norm, ffn2_norm, ffn2_w_gate_up,
           ffn2_w_down, conv_w_in, conv_a_kernel, conv_b_kernel, conv_b_bias, conv_b_ln_gain,
           conv_b_ln_bias, conv_w_out, attn_w_qkv, attn_w_o, final_norm):
    bsz, seq, d = x.shape
    n_tok = bsz * seq
    rows = lambda v: v.reshape(v.shape[0], 1, v.shape[1])
    bf16 = lambda w: w.astype(BF16)
    slopes = jnp.asarray(_alibi_slopes())
    g_final = final_norm.reshape(1, -1)
    ffn1_norm, mix_norm, ffn2_norm = rows(ffn1_norm), rows(mix_norm), rows(ffn2_norm)
    conv_b_bias, conv_b_ln_gain, conv_b_ln_bias = (
        rows(conv_b_bias), rows(conv_b_ln_gain), rows(conv_b_ln_bias))
    ffn1_wgu, ffn1_wd = ffn1_w_gate_up, ffn1_w_down
    ffn2_wgu, ffn2_wd = ffn2_w_gate_up, ffn2_w_down
    conv_w_in, conv_w_out = bf16(conv_w_in), bf16(conv_w_out)
    attn_w_qkv, attn_w_o = bf16(attn_w_qkv), bf16(attn_w_o)

    for layer in range(DEPTH):
        x = _ffn(x.reshape(n_tok, d), ffn1_norm, ffn1_wgu, ffn1_wd, g_final,
                 layer=layer, final_norm=False).reshape(bsz, seq, d)
        i = layer // 2
        last = layer == DEPTH - 1
        if layer % 2 == 0:
            x = _conv_mixer(x, mix_norm, conv_w_in, conv_a_kernel, conv_b_kernel, conv_b_bias,
                            conv_b_ln_gain, conv_b_ln_bias, conv_w_out, layer=layer, w_layer=i)
            x = _ffn(x.reshape(n_tok, d), ffn2_norm, ffn2_wgu, ffn2_wd, g_final,
                     layer=layer, final_norm=last).reshape(bsz, seq, d)
        else:
            qkv = _norm_proj(x, mix_norm, attn_w_qkv, layer=layer, w_layer=i)
            o = _attention(slopes, qkv)
            x = _attn_out_ffn(x, o, attn_w_o, ffn2_norm, ffn2_wgu, ffn2_wd, g_final,
                              layer=layer, w_layer=i, final_norm=last)
    return x
```

```python
import functools
import math

import numpy as np
import jax
import jax.numpy as jnp
from jax import lax
from jax.experimental import pallas as pl
from jax.experimental.pallas import tpu as pltpu

D_MODEL = 1024
DEPTH = 4
D_FF = 2816
RMS_EPS = 1e-6
LN_EPS = 1e-5
SC_WIDTH = 512
SC_KERNEL = 3
CM_WIDTH = 512
CM_KERNEL = 31
IN_AB = 3 * SC_WIDTH + 2 * CM_WIDTH
N_HEADS = 16
HEAD_DIM = 64
ATTN_WIDTH = N_HEADS * HEAD_DIM
DILATED_BRANCHES = ((128, 1), (512, 4), (2048, 16))
ATTN_BLOCK = 128
NEG_INF = -1e30

V7X_LANES = 128
V7X_SUBLANES = 8
V7X_MXU_COLS = 256
V7X_VMEM_BYTES = 64 * 1024 * 1024

F32 = jnp.float32
BF16 = jnp.bfloat16

TOKEN_TILE = 512
FFN_TILE = 1024
FF_CHUNK = V7X_MXU_COLS
FFN_STAGE_SLOTS = 3
CONV_ROWS = 64
CONV_A_HALO = V7X_SUBLANES
CONV_B_HALO = 32
HEADS_PER_STEP = V7X_LANES // HEAD_DIM
ATTN_GROUP = 4
ATTN_CLASSES = 4
ATTN_UNROLL = {1: 5, 4: 3, 16: 4}
LOG2_E = math.log2(math.e)


def _vmem_limit(estimate_bytes):
    return int(min(V7X_VMEM_BYTES - 8 * 1024 * 1024, estimate_bytes))


def _rms_norm(x, g):
    return x * lax.rsqrt(jnp.mean(x * x, axis=-1, keepdims=True) + RMS_EPS) * g


def _sigmoid(x):
    return 1.0 / (1.0 + jnp.exp(-x))


def _resident(shape, layer=None):
    zeros = (0,) * len(shape)
    if layer is None:
        return pl.BlockSpec(shape, lambda *_: zeros, pipeline_mode=pl.Buffered(1))
    return pl.BlockSpec((None,) + tuple(shape), lambda *_: (layer,) + zeros,
                        pipeline_mode=pl.Buffered(1))


def _ffn_tile(x, g_ref, wgu_ref, wd_ref, gf_ref, a_scr, final_norm, before_chunk=None):
    h = _rms_norm(x, g_ref[...]).astype(BF16)
    for c in range(D_FF // FF_CHUNK):
        lo = c * FF_CHUNK
        if before_chunk is not None:
            before_chunk(c)
        gate = jnp.dot(h, wgu_ref[:, lo:lo + FF_CHUNK], preferred_element_type=F32)
        up = jnp.dot(h, wgu_ref[:, D_FF + lo:D_FF + lo + FF_CHUNK], preferred_element_type=F32)
        a_scr[:, lo:lo + FF_CHUNK] = (gate * _sigmoid(gate) * up).astype(BF16)
    y = x + 0.5 * jnp.dot(a_scr[...], wd_ref[...], preferred_element_type=F32)
    if final_norm:
        y = _rms_norm(y, gf_ref[...])
    return y


def _ffn_weight_scratch():
    stage = lambda shape: pltpu.VMEM((FFN_STAGE_SLOTS,) + shape, F32)
    return [pltpu.VMEM((D_MODEL, 2 * D_FF), BF16),
            pltpu.VMEM((D_FF, D_MODEL), BF16),
            stage((D_MODEL, FF_CHUNK)), stage((D_MODEL, FF_CHUNK)), stage((FF_CHUNK, D_MODEL)),
            pltpu.SemaphoreType.DMA((FFN_STAGE_SLOTS, 3))]


def _ffn_weight_streamer(wgu_hbm, wd_hbm, scratch):
    wgu_bf, wd_bf, stage_g, stage_u, stage_d, sems = scratch
    n_chunks = D_FF // FF_CHUNK

    def parts(c):
        slot, lo = c % FFN_STAGE_SLOTS, c * FF_CHUNK
        gate_cols, up_cols = pl.ds(lo, FF_CHUNK), pl.ds(D_FF + lo, FF_CHUNK)
        down_rows = pl.ds(lo, FF_CHUNK)
        return [
            (pltpu.make_async_copy(wgu_hbm.at[:, gate_cols], stage_g.at[slot], sems.at[slot, 0]),
             stage_g, wgu_bf.at[:, gate_cols]),
            (pltpu.make_async_copy(wgu_hbm.at[:, up_cols], stage_u.at[slot], sems.at[slot, 1]),
             stage_u, wgu_bf.at[:, up_cols]),
            (pltpu.make_async_copy(wd_hbm.at[down_rows, :], stage_d.at[slot], sems.at[slot, 2]),
             stage_d, wd_bf.at[down_rows, :]),
        ]

    def fetch(c):
        for copy, _, _ in parts(c):
            copy.start()

    ahead = FFN_STAGE_SLOTS - 1

    def before_chunk(c):
        if c == 0:
            for first in range(ahead):
                fetch(first)
        if c + ahead < n_chunks:
            fetch(c + ahead)
        for copy, stage, dst in parts(c):
            copy.wait()
            dst[...] = stage[c % FFN_STAGE_SLOTS].astype(BF16)

    return before_chunk


def _ffn_tile_streaming(x_ref_tile, o_ref_tile, first_step, wgu_hbm, wd_hbm, scratch,
                        g_ref, gf_ref, a_scr, final_norm):
    wgu_bf, wd_bf = scratch[0], scratch[1]

    @pl.when(first_step)
    def _with_weight_load():
        o_ref_tile(_ffn_tile(x_ref_tile(), g_ref, wgu_bf, wd_bf, gf_ref, a_scr, final_norm,
                             before_chunk=_ffn_weight_streamer(wgu_hbm, wd_hbm, scratch)))

    @pl.when(jnp.logical_not(first_step))
    def _weights_resident():
        o_ref_tile(_ffn_tile(x_ref_tile(), g_ref, wgu_bf, wd_bf, gf_ref, a_scr, final_norm))


def _ffn_kernel(x_ref, g_ref, wgu_hbm, wd_hbm, gf_ref, o_ref, a_scr, *weight_scratch,
                layer, final_norm):
    def write(y):
        o_ref[...] = y
    _ffn_tile_streaming(lambda: x_ref[...], write, pl.program_id(0) == 0,
                        wgu_hbm.at[layer], wd_hbm.at[layer], weight_scratch,
                        g_ref, gf_ref, a_scr, final_norm)


def _ffn_vmem_estimate(tm):
    return (4 * tm * D_MODEL * 4
            + 3 * D_MODEL * D_FF * 2
            + FFN_STAGE_SLOTS * 3 * D_MODEL * FF_CHUNK * 4
            + tm * D_FF * 2
            + 6 * tm * D_MODEL * 4)


def _ffn(x2d, g, wgu, wd, g_final, *, layer, final_norm):
    n_tok = x2d.shape[0]
    tm = FFN_TILE
    est = _ffn_vmem_estimate(tm)
    return pl.pallas_call(
        functools.partial(_ffn_kernel, layer=layer, final_norm=final_norm),
        out_shape=jax.ShapeDtypeStruct((n_tok, D_MODEL), F32),
        grid=(n_tok // tm,),
        in_specs=[
            pl.BlockSpec((tm, D_MODEL), lambda i: (i, 0)),
            _resident((1, D_MODEL), layer),
            pl.BlockSpec(memory_space=pl.ANY),
            pl.BlockSpec(memory_space=pl.ANY),
            _resident((1, D_MODEL)),
        ],
        out_specs=pl.BlockSpec((tm, D_MODEL), lambda i: (i, 0)),
        scratch_shapes=[pltpu.VMEM((tm, D_FF), BF16)] + _ffn_weight_scratch(),
        compiler_params=pltpu.CompilerParams(
            dimension_semantics=("arbitrary",), vmem_limit_bytes=_vmem_limit(est)),
        name="ffn_final" if final_norm else "ffn",
    )(x2d, g, wgu, wd, g_final)


def _class_rows(c, tm):
    return pl.ds(c, tm // ATTN_CLASSES, stride=ATTN_CLASSES)


def _lane_block(j):
    return slice(j * V7X_LANES, (j + 1) * V7X_LANES)


def _to_class_major(tile, scr, tm):
    n_blocks = tile.shape[1] // V7X_LANES
    for j in range(n_blocks):
        scr[j] = tile[:, _lane_block(j)]
    return jnp.concatenate(
        [jnp.concatenate([scr[j, _class_rows(c, tm), :] for j in range(n_blocks)], axis=1)
         for c in range(ATTN_CLASSES)], axis=0)


def _from_class_major(tile, scr, tm):
    n_blocks = tile.shape[1] // V7X_LANES
    per_class = tm // ATTN_CLASSES
    for j in range(n_blocks):
        for c in range(ATTN_CLASSES):
            scr[j, _class_rows(c, tm), :] = tile[c * per_class:(c + 1) * per_class, _lane_block(j)]
    return jnp.concatenate([scr[j] for j in range(n_blocks)], axis=1)


def _norm_proj_kernel(x_ref, g_ref, w_ref, o_ref, scr, *, tm):
    per_class = tm // ATTN_CLASSES
    h = _rms_norm(_to_class_major(x_ref[0], scr, tm), g_ref[...]).astype(BF16)
    y = jnp.dot(h, w_ref[...], preferred_element_type=F32)
    for c in range(ATTN_CLASSES):
        o_ref[c, 0] = y[c * per_class:(c + 1) * per_class]


def _norm_proj(x, g, w, *, layer, w_layer):
    bsz, seq, _ = x.shape
    n_out = w.shape[-1]
    tm = TOKEN_TILE
    per_class = tm // ATTN_CLASSES
    est = (2 * tm * D_MODEL * 4 + 2 * tm * n_out * 4 + D_MODEL * n_out * 2 + 2 * tm * n_out * 4)
    return pl.pallas_call(
        functools.partial(_norm_proj_kernel, tm=tm),
        out_shape=jax.ShapeDtypeStruct((ATTN_CLASSES, bsz, seq // ATTN_CLASSES, n_out), F32),
        grid=(bsz, seq // tm),
        in_specs=[
            pl.BlockSpec((1, tm, D_MODEL), lambda b, i: (b, i, 0)),
            _resident((1, D_MODEL), layer),
            _resident(w.shape[1:], w_layer),
        ],
        out_specs=pl.BlockSpec((ATTN_CLASSES, 1, per_class, n_out), lambda b, i: (0, b, i, 0)),
        scratch_shapes=[pltpu.VMEM((D_MODEL // V7X_LANES, tm, V7X_LANES), F32)],
        compiler_params=pltpu.CompilerParams(
            dimension_semantics=("arbitrary",) * 2, vmem_limit_bytes=_vmem_limit(est)),
        name="norm_proj",
    )(x, g, w)


def _attn_out_ffn_kernel(x_ref, y_ref, wo_ref, g_ref, wgu_hbm, wd_hbm, gf_ref, o_ref,
                         a_scr, cm_scr, *weight_scratch, tm, layer, final_norm):
    first_step = (pl.program_id(0) == 0) & (pl.program_id(1) == 0)

    def read():
        y = jnp.concatenate([y_ref[c, 0] for c in range(ATTN_CLASSES)], axis=0).astype(BF16)
        r = jnp.dot(y, wo_ref[...], preferred_element_type=F32)
        return x_ref[0] + _from_class_major(r, cm_scr, tm)

    def write(y):
        o_ref[0] = y
    _ffn_tile_streaming(read, write, first_step, wgu_hbm.at[layer], wd_hbm.at[layer],
                        weight_scratch, g_ref, gf_ref, a_scr, final_norm)


def _attn_out_ffn(x, y, w_o, g, wgu, wd, g_final, *, layer, w_layer, final_norm):
    bsz, seq, _ = x.shape
    tm = TOKEN_TILE
    per_class = tm // ATTN_CLASSES
    est = (_ffn_vmem_estimate(tm) + ATTN_WIDTH * D_MODEL * 2
           + 2 * tm * ATTN_WIDTH * 4
           + 3 * tm * D_MODEL * 4)
    x_tile = pl.BlockSpec((1, tm, D_MODEL), lambda b, i: (b, i, 0))
    return pl.pallas_call(
        functools.partial(_attn_out_ffn_kernel, tm=tm, layer=layer, final_norm=final_norm),
        out_shape=jax.ShapeDtypeStruct(x.shape, F32),
        grid=(bsz, seq // tm),
        in_specs=[
            x_tile,
            pl.BlockSpec((ATTN_CLASSES, 1, per_class, y.shape[-1]), lambda b, i: (0, b, i, 0)),
            _resident(w_o.shape[1:], w_layer),
            _resident((1, D_MODEL), layer),
            pl.BlockSpec(memory_space=pl.ANY),
            pl.BlockSpec(memory_space=pl.ANY),
            _resident((1, D_MODEL)),
        ],
        out_specs=x_tile,
        scratch_shapes=[pltpu.VMEM((tm, D_FF), BF16),
                        pltpu.VMEM((D_MODEL // V7X_LANES, tm, V7X_LANES), F32)]
        + _ffn_weight_scratch(),
        compiler_params=pltpu.CompilerParams(
            dimension_semantics=("arbitrary",) * 2, vmem_limit_bytes=_vmem_limit(est)),
        name="attn_out_ffn_final" if final_norm else "attn_out_ffn",
    )(x, y, w_o, g, wgu, wd, g_final)


def _conv_mixer_kernel(x_ref, g_ref, w_in_ref, ka_ref, kb_ref, bb_ref, lng_ref, lnb_ref,
                       w_out_ref, o_ref, cx_scr, u_scr, cv_scr, *, tm):
    ha, hb = CONV_A_HALO, CONV_B_HALO

    @pl.when(pl.program_id(1) == 0)
    def _start_of_sequence():
        cx_scr[0:ha, :] = jnp.zeros((ha, SC_WIDTH), F32)
        u_scr[0:hb, :] = jnp.zeros((hb, CM_WIDTH), F32)

    x = x_ref[0]
    h = _rms_norm(x, g_ref[...]).astype(BF16)

    def proj(lo, width):
        return jnp.dot(h, w_in_ref[:, lo:lo + width], preferred_element_type=F32)

    u_scr[hb:hb + tm, :] = proj(3 * SC_WIDTH, CM_WIDTH) * _sigmoid(
        proj(3 * SC_WIDTH + CM_WIDTH, CM_WIDTH))

    first_off = hb - (CM_KERNEL - 1)

    def conv_b_rows(r0):
        for j in range(CM_WIDTH // V7X_LANES):
            lanes = slice(j * V7X_LANES, (j + 1) * V7X_LANES)
            n_win = CONV_ROWS + hb
            window = u_scr[pl.ds(r0, n_win), lanes]
            acc = jnp.zeros((CONV_ROWS, V7X_LANES), F32)
            for sub in range(V7X_SUBLANES):
                offs = [o for o in range(first_off, hb + 1) if o % V7X_SUBLANES == sub]
                shifted = pltpu.roll(window, n_win - sub, axis=0) if sub else window
                for o in offs:
                    k = o - first_off
                    acc = acc + kb_ref[k:k + 1, lanes] * shifted[o - sub:o - sub + CONV_ROWS, :]
            cv_scr[pl.ds(r0, CONV_ROWS), lanes] = acc

    a_b = []

    def mixer_a_cols(lo):
        a_b.append(proj(lo, V7X_MXU_COLS))
        cx_scr[ha:ha + tm, lo:lo + V7X_MXU_COLS] = (
            proj(SC_WIDTH + lo, V7X_MXU_COLS) * proj(2 * SC_WIDTH + lo, V7X_MXU_COLS))

    def mixer_a_output():
        conv_a = ka_ref[0:1, :] * cx_scr[ha - 2:ha - 2 + tm, :]
        conv_a = conv_a + ka_ref[1:2, :] * cx_scr[ha - 1:ha - 1 + tm, :]
        conv_a = conv_a + ka_ref[2:3, :] * cx_scr[ha:ha + tm, :]
        y_a = (jnp.concatenate(a_b, axis=1) * conv_a).astype(BF16)
        return jnp.dot(y_a, w_out_ref[0:SC_WIDTH, :], preferred_element_type=F32)

    out_a = None
    for idx, r0 in enumerate(range(0, tm, CONV_ROWS)):
        conv_b_rows(r0)
        if idx < SC_WIDTH // V7X_MXU_COLS:
            mixer_a_cols(idx * V7X_MXU_COLS)
        elif out_a is None:
            out_a = mixer_a_output()

    u = cv_scr[...] + bb_ref[...]
    mu = jnp.mean(u, axis=-1, keepdims=True)
    uc = u - mu
    un = uc * lax.rsqrt(jnp.mean(uc * uc, axis=-1, keepdims=True) + LN_EPS)
    un = un * lng_ref[...] + lnb_ref[...]
    y_b = (un * _sigmoid(un)).astype(BF16)
    o_ref[0] = x + out_a + jnp.dot(y_b, w_out_ref[SC_WIDTH:SC_WIDTH + CM_WIDTH, :],
                                   preferred_element_type=F32)

    cx_scr[0:ha, :] = cx_scr[tm:tm + ha, :]
    u_scr[0:hb, :] = u_scr[tm:tm + hb, :]


def _conv_mixer(x, g, w_in, ka, kb, bb, lng, lnb, w_out, *, layer, w_layer):
    bsz, seq, _ = x.shape
    tm = TOKEN_TILE
    est = (4 * tm * D_MODEL * 4 + (D_MODEL * IN_AB + D_MODEL * D_MODEL) * 2
           + (3 * tm + 64) * SC_WIDTH * 4 + tm * D_MODEL * 2
           + 8 * tm * D_MODEL * 4)
    tile = pl.BlockSpec((1, tm, D_MODEL), lambda b, t: (b, t, 0))
    return pl.pallas_call(
        functools.partial(_conv_mixer_kernel, tm=tm),
        out_shape=jax.ShapeDtypeStruct(x.shape, F32),
        grid=(bsz, seq // tm),
        in_specs=[
            tile,
            _resident((1, D_MODEL), layer),
            _resident(w_in.shape[1:], w_layer),
            _resident(ka.shape[1:], w_layer),
            _resident(kb.shape[1:], w_layer),
            _resident((1, CM_WIDTH), w_layer),
            _resident((1, CM_WIDTH), w_layer),
            _resident((1, CM_WIDTH), w_layer),
            _resident(w_out.shape[1:], w_layer),
        ],
        out_specs=tile,
        scratch_shapes=[
            pltpu.VMEM((CONV_A_HALO + tm, SC_WIDTH), F32),
            pltpu.VMEM((CONV_B_HALO + tm, CM_WIDTH), F32),
            pltpu.VMEM((tm, CM_WIDTH), F32),
        ],
        compiler_params=pltpu.CompilerParams(
            dimension_semantics=("arbitrary", "arbitrary"),
            vmem_limit_bytes=_vmem_limit(est)),
        name="conv_mixer",
    )(x, g, w_in, ka, kb, bb, lng, lnb, w_out)


def _attention_kernel(slopes_ref, q_ref, k_ref, v_ref, o_ref, m_scr, l_scr, *, seq):
    blk, grp, ncls = ATTN_BLOCK, ATTN_GROUP, ATTN_CLASSES
    sub = blk // ncls
    pair = pl.program_id(1)
    slope = [slopes_ref[HEADS_PER_STEP * pair + h] for h in range(HEADS_PER_STEP)]
    first_head = lax.broadcasted_iota(jnp.int32, (1, V7X_LANES), 1) < HEAD_DIM
    head_lanes = [first_head, jnp.logical_not(first_head)]
    nt_dims = (((1,), (1,)), ((), ()))
    q_scale = LOG2_E / math.sqrt(HEAD_DIM)

    q_cls = [q_ref.at[c, 0] for c in range(ncls)]
    k_cls = [k_ref.at[c, 0] for c in range(ncls)]
    v_cls = [v_ref.at[c, 0] for c in range(ncls)]
    o_cls = [o_ref.at[c, 0] for c in range(ncls)]
    m_cls = [m_scr.at[c] for c in range(ncls)]
    l_cls = [l_scr.at[c] for c in range(ncls)]

    def class_rows(c, j):
        def load(refs, first, n):
            start = pl.multiple_of((j * grp + first) * blk, blk)
            return refs[c][pl.ds(start, n * blk), :]

        def store(refs, value):
            refs[c][pl.ds(pl.multiple_of(j * grp * blk, blk), grp * blk), :] = value
        return load, store

    def subclass_rows(c, a, step):
        def load(refs, first, n):
            assert first == 0 and n == grp
            return refs[c][pl.ds(a, grp * blk, stride=step), :]

        def store(refs, value):
            refs[c][pl.ds(a, grp * blk, stride=step), :] = value
        return load, store

    def natural_rows(j):
        def load(refs, first, n):
            start = pl.multiple_of((j * grp + first) * sub, sub)
            per_class = [refs[c][pl.ds(start, n * sub), :] for c in range(ncls)]
            return jnp.concatenate(
                [per_class[c][b * sub:(b + 1) * sub] for b in range(n) for c in range(ncls)], axis=0)

        def store(refs, value):
            start = pl.multiple_of(j * grp * sub, sub)
            for c in range(ncls):
                refs[c][pl.ds(start, grp * sub), :] = jnp.concatenate(
                    [value[b * blk + c * sub:b * blk + (c + 1) * sub] for b in range(grp)], axis=0)
        return load, store

    def run_group(access, first_group, bias_one, bias_two, first_branch, last_branch):
        load, store = access
        q = load(q_cls, 0, grp) * q_scale
        k_first, k_blocks = (0, grp) if first_group else (-1, grp + 1)
        kb = load(k_cls, k_first, k_blocks).astype(BF16)
        vb = load(v_cls, k_first, k_blocks).astype(BF16)
        vb = jnp.concatenate([vb, jnp.ones(vb.shape, BF16)], axis=1)
        qh = [jnp.where(head_lanes[h], q, 0.0).astype(BF16) for h in range(HEADS_PER_STEP)]

        ms, ls, pvs = [], [], []
        for u in range(grp):
            if first_group and u == 0:
                keys, bias = slice(0, blk), bias_one
            else:
                lo = u - 1 if first_group else u
                keys, bias = slice(lo * blk, (lo + 2) * blk), bias_two
            rows = slice(u * blk, (u + 1) * blk)
            lhs = jnp.concatenate([qh[h][rows] for h in range(HEADS_PER_STEP)], axis=0)
            s = lax.dot_general(lhs, kb[keys], nt_dims, preferred_element_type=F32) + bias
            m = jnp.max(s, axis=-1, keepdims=True)
            p = jnp.exp2(s - m).astype(BF16)
            r = jnp.dot(p, vb[keys], preferred_element_type=F32)
            mb = jnp.broadcast_to(m, (HEADS_PER_STEP * blk, V7X_LANES))
            ms.append(jnp.where(first_head, mb[:blk], mb[blk:]))
            pvs.append(jnp.where(first_head, r[:blk, :V7X_LANES], r[blk:, :V7X_LANES]))
            ls.append(jnp.where(first_head, r[:blk, V7X_LANES:], r[blk:, V7X_LANES:]))
        m_new = jnp.concatenate(ms, axis=0)
        l_new = jnp.concatenate(ls, axis=0)
        acc = jnp.concatenate(pvs, axis=0)

        if not first_branch:
            m_old, l_old, acc_old = load(m_cls, 0, grp), load(l_cls, 0, grp), load(o_cls, 0, grp)
            top = jnp.maximum(m_old, m_new)
            w_old = jnp.exp2(m_old - top)
            w_new = jnp.exp2(m_new - top)
            l_new = w_old * l_old + w_new * l_new
            acc = w_old * acc_old + w_new * acc
            m_new = top
        if last_branch:
            store(o_cls, acc / l_new)
        else:
            store(m_cls, m_new)
            store(l_cls, l_new)
            store(o_cls, acc)

    def bias_tables(dil, natural):
        def position(idx):
            if not natural:
                return idx
            inside = idx & (blk - 1)
            return (idx - inside) + ncls * (inside & (sub - 1)) + (inside >> int(math.log2(sub)))
        iq = position(lax.broadcasted_iota(jnp.int32, (blk, 2 * blk), 0))
        ik = position(lax.broadcasted_iota(jnp.int32, (blk, 2 * blk), 1))
        rel = blk + iq - ik
        valid = (rel >= 0) & (rel <= ATTN_BLOCK)
        dist = (dil * rel).astype(F32)
        two = jnp.concatenate(
            [jnp.where(valid, (-LOG2_E * slope[h]) * dist, NEG_INF) for h in range(HEADS_PER_STEP)],
            axis=0)
        return two[:, blk:], two

    rows_per_class = seq // ncls
    for branch, (window, dil) in enumerate(DILATED_BRANCHES):
        assert window // dil == ATTN_BLOCK
        first_branch, last_branch = branch == 0, branch == len(DILATED_BRANCHES) - 1
        bias_one, bias_two = bias_tables(dil, natural=(dil == 1))
        run = functools.partial(run_group, bias_one=bias_one, bias_two=bias_two,
                                first_branch=first_branch, last_branch=last_branch)

        def sweep(make_access, n_groups, run=run):
            run(make_access(0), True)
            if n_groups > 1:
                def later(j, carry):
                    run(make_access(j), False)
                    return carry
                lax.fori_loop(1, n_groups, later, 0, unroll=ATTN_UNROLL[dil])

        if dil == 1:
            sweep(natural_rows, seq // (grp * blk))
        elif dil == ncls:
            for c in range(ncls):
                sweep(functools.partial(class_rows, c), rows_per_class // (grp * blk))
        else:
            step = dil // ncls
            assert rows_per_class // step == grp * blk
            for c in range(ncls):
                def subclass(a, carry, c=c, step=step, run=run):
                    run(subclass_rows(c, a, step), True)
                    return carry
                lax.fori_loop(0, step, subclass, 0, unroll=ATTN_UNROLL[dil])


def _attention(slopes, qkv):
    ncls, bsz, rows, _ = qkv.shape
    n_pairs = N_HEADS // HEADS_PER_STEP
    block_bytes = ncls * rows * V7X_LANES * 4
    est = (2 * 4 * block_bytes
           + 2 * block_bytes
           + 6 * 1024 * 1024)

    def cols(offset):
        return pl.BlockSpec((ncls, 1, rows, V7X_LANES), lambda b, p: (0, b, 0, offset + p))

    return pl.pallas_call(
        functools.partial(_attention_kernel, seq=ncls * rows),
        out_shape=jax.ShapeDtypeStruct((ncls, bsz, rows, ATTN_WIDTH), F32),
        grid=(bsz, n_pairs),
        in_specs=[
            pl.BlockSpec(memory_space=pltpu.SMEM),
            cols(0), cols(n_pairs), cols(2 * n_pairs),
        ],
        out_specs=cols(0),
        scratch_shapes=[pltpu.VMEM((ncls, rows, V7X_LANES), F32),
                        pltpu.VMEM((ncls, rows, V7X_LANES), F32)],
        compiler_params=pltpu.CompilerParams(
            dimension_semantics=("arbitrary", "arbitrary"),
            vmem_limit_bytes=_vmem_limit(est)),
        name="dilated_attention",
    )(slopes, qkv, qkv, qkv)


def _alibi_slopes():
    return np.array([2.0 ** (-8.0 * (i + 1) / N_HEADS) for i in range(N_HEADS)], dtype=np.float32)


def kernel(x, ffn1_norm, ffn1_w_gate_up, ffn1_w_down, mix_norm, ffn2_norm, ffn2_w_gate_up,
           ffn2_w_down, conv_w_in, conv_a_kernel, conv_b_kernel, conv_b_bias, conv_b_ln_gain,
           conv_b_ln_bias, conv_w_out, attn_w_qkv, attn_w_o, final_norm):
    bsz, seq, d = x.shape
    n_tok = bsz * seq
    rows = lambda v: v.reshape(v.shape[0], 1, v.shape[1])
    bf16 = lambda w: w.astype(BF16)
    slopes = jnp.asarray(_alibi_slopes())
    g_final = final_norm.reshape(1, -1)
    ffn1_norm, mix_norm, ffn2_norm = rows(ffn1_norm), rows(mix_norm), rows(ffn2_norm)
    conv_b_bias, conv_b_ln_gain, conv_b_ln_bias = (
        rows(conv_b_bias), rows(conv_b_ln_gain), rows(conv_b_ln_bias))
    ffn1_wgu, ffn1_wd = ffn1_w_gate_up, ffn1_w_down
    ffn2_wgu, ffn2_wd = ffn2_w_gate_up, ffn2_w_down
    conv_w_in, conv_w_out = bf16(conv_w_in), bf16(conv_w_out)
    attn_w_qkv, attn_w_o = bf16(attn_w_qkv), bf16(attn_w_o)

    for layer in range(DEPTH):
        x = _ffn(x.reshape(n_tok, d), ffn1_norm, ffn1_wgu, ffn1_wd, g_final,
                 layer=layer, final_norm=False).reshape(bsz, seq, d)
        i = layer // 2
        last = layer == DEPTH - 1
        if layer % 2 == 0:
            x = _conv_mixer(x, mix_norm, conv_w_in, conv_a_kernel, conv_b_kernel, conv_b_bias,
                            conv_b_ln_gain, conv_b_ln_bias, conv_w_out, layer=layer, w_layer=i)
            x = _ffn(x.reshape(n_tok, d), ffn2_norm, ffn2_wgu, ffn2_wd, g_final,
                     layer=layer, final_norm=last).reshape(bsz, seq, d)
        else:
            qkv = _norm_proj(x, mix_norm, attn_w_qkv, layer=layer, w_layer=i)
            o = _attention(slopes, qkv)
            x = _attn_out_ffn(x, o, attn_w_o, ffn2_norm, ffn2_wgu, ffn2_wd, g_final,
                              layer=layer, w_layer=i, final_norm=last)
    return x
```

```python
import functools
import math

import numpy as np
import jax
import jax.numpy as jnp
from jax import lax
from jax.experimental import pallas as pl
from jax.experimental.pallas import tpu as pltpu

D_MODEL = 1024
DEPTH = 4
D_FF = 2816
RMS_EPS = 1e-6
LN_EPS = 1e-5
SC_WIDTH = 512
SC_KERNEL = 3
CM_WIDTH = 512
CM_KERNEL = 31
IN_AB = 3 * SC_WIDTH + 2 * CM_WIDTH
N_HEADS = 16
HEAD_DIM = 64
ATTN_WIDTH = N_HEADS * HEAD_DIM
DILATED_BRANCHES = ((128, 1), (512, 4), (2048, 16))
ATTN_BLOCK = 128
NEG_INF = -1e30

V7X_LANES = 128
V7X_SUBLANES = 8
V7X_MXU_COLS = 256
V7X_VMEM_BYTES = 64 * 1024 * 1024

F32 = jnp.float32
BF16 = jnp.bfloat16

TOKEN_TILE = 512
FFN_TILE = 1024
FF_CHUNK = V7X_MXU_COLS
FFN_STAGE_SLOTS = 3
CONV_ROWS = 64
CONV_A_HALO = V7X_SUBLANES
CONV_B_HALO = 32
HEADS_PER_STEP = V7X_LANES // HEAD_DIM
ATTN_GROUP = 4
ATTN_CLASSES = 4
ATTN_UNROLL = {1: 5, 4: 3, 16: 4}
LOG2_E = math.log2(math.e)


def _vmem_limit(estimate_bytes):
    return int(min(V7X_VMEM_BYTES - 8 * 1024 * 1024, estimate_bytes))


def _rms_norm(x, g):
    return x * lax.rsqrt(jnp.mean(x * x, axis=-1, keepdims=True) + RMS_EPS) * g


def _sigmoid(x):
    return 1.0 / (1.0 + jnp.exp(-x))


def _resident(shape, layer=None):
    zeros = (0,) * len(shape)
    if layer is None:
        return pl.BlockSpec(shape, lambda *_: zeros, pipeline_mode=pl.Buffered(1))
    return pl.BlockSpec((None,) + tuple(shape), lambda *_: (layer,) + zeros,
                        pipeline_mode=pl.Buffered(1))


def _ffn_tile(x, g_ref, wgu_ref, wd_ref, gf_ref, a_scr, final_norm, before_chunk=None):
    h = _rms_norm(x, g_ref[...]).astype(BF16)
    for c in range(D_FF // FF_CHUNK):
        lo = c * FF_CHUNK
        if before_chunk is not None:
            before_chunk(c)
        gate = jnp.dot(h, wgu_ref[:, lo:lo + FF_CHUNK], preferred_element_type=F32)
        up = jnp.dot(h, wgu_ref[:, D_FF + lo:D_FF + lo + FF_CHUNK], preferred_element_type=F32)
        a_scr[:, lo:lo + FF_CHUNK] = (gate * _sigmoid(gate) * up).astype(BF16)
    y = x + 0.5 * jnp.dot(a_scr[...], wd_ref[...], preferred_element_type=F32)
    if final_norm:
        y = _rms_norm(y, gf_ref[...])
    return y


def _ffn_weight_scratch():
    stage = lambda shape: pltpu.VMEM((FFN_STAGE_SLOTS,) + shape, F32)
    return [pltpu.VMEM((D_MODEL, 2 * D_FF), BF16),
            pltpu.VMEM((D_FF, D_MODEL), BF16),
            stage((D_MODEL, FF_CHUNK)), stage((D_MODEL, FF_CHUNK)), stage((FF_CHUNK, D_MODEL)),
            pltpu.SemaphoreType.DMA((FFN_STAGE_SLOTS, 3))]


def _ffn_weight_streamer(wgu_hbm, wd_hbm, scratch, down_first):
    wgu_bf, wd_bf, stage_g, stage_u, stage_d, sems = scratch
    n_chunks = D_FF // FF_CHUNK

    def parts(c):
        slot, lo = c % FFN_STAGE_SLOTS, c * FF_CHUNK
        gate_cols, up_cols = pl.ds(lo, FF_CHUNK), pl.ds(D_FF + lo, FF_CHUNK)
        down_rows = pl.ds(lo, FF_CHUNK)
        return [
            (pltpu.make_async_copy(wgu_hbm.at[:, gate_cols], stage_g.at[slot], sems.at[slot, 0]),
             stage_g, wgu_bf.at[:, gate_cols]),
            (pltpu.make_async_copy(wgu_hbm.at[:, up_cols], stage_u.at[slot], sems.at[slot, 1]),
             stage_u, wgu_bf.at[:, up_cols]),
            (pltpu.make_async_copy(wd_hbm.at[down_rows, :], stage_d.at[slot], sems.at[slot, 2]),
             stage_d, wd_bf.at[down_rows, :]),
        ]

    ahead = FFN_STAGE_SLOTS - 1

    def ring_step(pieces):
        def fetch(c):
            for copy, _, _ in parts(c)[pieces]:
                copy.start()

        def step(c):
            if c == 0:
                for first in range(ahead):
                    fetch(first)
            if c + ahead < n_chunks:
                fetch(c + ahead)
            for copy, stage, dst in parts(c)[pieces]:
                copy.wait()
                dst[...] = stage[c % FFN_STAGE_SLOTS].astype(BF16)
        return step

    if not down_first:
        return ring_step(slice(0, 3))
    load_down = ring_step(slice(2, 3))
    for c in range(n_chunks):
        load_down(c)
    return ring_step(slice(0, 2))


def _ffn_tile_streaming(x_ref_tile, o_ref_tile, first_step, wgu_hbm, wd_hbm, scratch,
                        g_ref, gf_ref, a_scr, final_norm, down_first=False):
    wgu_bf, wd_bf = scratch[0], scratch[1]

    @pl.when(first_step)
    def _with_weight_load():
        before_chunk = _ffn_weight_streamer(wgu_hbm, wd_hbm, scratch, down_first)
        o_ref_tile(_ffn_tile(x_ref_tile(), g_ref, wgu_bf, wd_bf, gf_ref, a_scr, final_norm,
                             before_chunk=before_chunk))

    @pl.when(jnp.logical_not(first_step))
    def _weights_resident():
        o_ref_tile(_ffn_tile(x_ref_tile(), g_ref, wgu_bf, wd_bf, gf_ref, a_scr, final_norm))


def _ffn_kernel(x_ref, g_ref, wgu_hbm, wd_hbm, gf_ref, o_ref, a_scr, *weight_scratch,
                layer, final_norm, down_first):
    def write(y):
        o_ref[...] = y
    _ffn_tile_streaming(lambda: x_ref[...], write, pl.program_id(0) == 0,
                        wgu_hbm.at[layer], wd_hbm.at[layer], weight_scratch,
                        g_ref, gf_ref, a_scr, final_norm, down_first)


def _ffn_vmem_estimate(tm):
    return (4 * tm * D_MODEL * 4
            + 3 * D_MODEL * D_FF * 2
            + FFN_STAGE_SLOTS * 3 * D_MODEL * FF_CHUNK * 4
            + tm * D_FF * 2
            + 6 * tm * D_MODEL * 4)


def _ffn(x2d, g, wgu, wd, g_final, *, layer, final_norm, down_first=False):
    n_tok = x2d.shape[0]
    tm = FFN_TILE
    est = _ffn_vmem_estimate(tm)
    return pl.pallas_call(
        functools.partial(_ffn_kernel, layer=layer, final_norm=final_norm,
                          down_first=down_first),
        out_shape=jax.ShapeDtypeStruct((n_tok, D_MODEL), F32),
        grid=(n_tok // tm,),
        in_specs=[
            pl.BlockSpec((tm, D_MODEL), lambda i: (i, 0)),
            _resident((1, D_MODEL), layer),
            pl.BlockSpec(memory_space=pl.ANY),
            pl.BlockSpec(memory_space=pl.ANY),
            _resident((1, D_MODEL)),
        ],
        out_specs=pl.BlockSpec((tm, D_MODEL), lambda i: (i, 0)),
        scratch_shapes=[pltpu.VMEM((tm, D_FF), BF16)] + _ffn_weight_scratch(),
        compiler_params=pltpu.CompilerParams(
            dimension_semantics=("arbitrary",), vmem_limit_bytes=_vmem_limit(est)),
        name="ffn_final" if final_norm else "ffn",
    )(x2d, g, wgu, wd, g_final)


def _class_rows(c, tm):
    return pl.ds(c, tm // ATTN_CLASSES, stride=ATTN_CLASSES)


def _lane_block(j):
    return slice(j * V7X_LANES, (j + 1) * V7X_LANES)


def _to_class_major(tile, scr, tm):
    n_blocks = tile.shape[1] // V7X_LANES
    for j in range(n_blocks):
        scr[j] = tile[:, _lane_block(j)]
    return jnp.concatenate(
        [jnp.concatenate([scr[j, _class_rows(c, tm), :] for j in range(n_blocks)], axis=1)
         for c in range(ATTN_CLASSES)], axis=0)


def _from_class_major(tile, scr, tm):
    n_blocks = tile.shape[1] // V7X_LANES
    per_class = tm // ATTN_CLASSES
    for j in range(n_blocks):
        for c in range(ATTN_CLASSES):
            scr[j, _class_rows(c, tm), :] = tile[c * per_class:(c + 1) * per_class, _lane_block(j)]
    return jnp.concatenate([scr[j] for j in range(n_blocks)], axis=1)


def _norm_proj_kernel(x_ref, g_ref, w_ref, o_ref, scr, *, tm):
    per_class = tm // ATTN_CLASSES
    h = _rms_norm(_to_class_major(x_ref[0], scr, tm), g_ref[...]).astype(BF16)
    y = jnp.dot(h, w_ref[...], preferred_element_type=F32)
    for c in range(ATTN_CLASSES):
        o_ref[c, 0] = y[c * per_class:(c + 1) * per_class]


def _norm_proj(x, g, w, *, layer, w_layer):
    bsz, seq, _ = x.shape
    n_out = w.shape[-1]
    tm = TOKEN_TILE
    per_class = tm // ATTN_CLASSES
    est = (2 * tm * D_MODEL * 4 + 2 * tm * n_out * 4 + D_MODEL * n_out * 2 + 2 * tm * n_out * 4)
    return pl.pallas_call(
        functools.partial(_norm_proj_kernel, tm=tm),
        out_shape=jax.ShapeDtypeStruct((ATTN_CLASSES, bsz, seq // ATTN_CLASSES, n_out), F32),
        grid=(bsz, seq // tm),
        in_specs=[
            pl.BlockSpec((1, tm, D_MODEL), lambda b, i: (b, i, 0)),
            _resident((1, D_MODEL), layer),
            _resident(w.shape[1:], w_layer),
        ],
        out_specs=pl.BlockSpec((ATTN_CLASSES, 1, per_class, n_out), lambda b, i: (0, b, i, 0)),
        scratch_shapes=[pltpu.VMEM((D_MODEL // V7X_LANES, tm, V7X_LANES), F32)],
        compiler_params=pltpu.CompilerParams(
            dimension_semantics=("arbitrary",) * 2, vmem_limit_bytes=_vmem_limit(est)),
        name="norm_proj",
    )(x, g, w)


def _attn_out_ffn_kernel(x_ref, y_ref, wo_ref, g_ref, wgu_hbm, wd_hbm, gf_ref, o_ref,
                         a_scr, cm_scr, *weight_scratch, tm, layer, final_norm):
    first_step = (pl.program_id(0) == 0) & (pl.program_id(1) == 0)

    def read():
        y = jnp.concatenate([y_ref[c, 0] for c in range(ATTN_CLASSES)], axis=0).astype(BF16)
        r = jnp.dot(y, wo_ref[...], preferred_element_type=F32)
        return x_ref[0] + _from_class_major(r, cm_scr, tm)

    def write(y):
        o_ref[0] = y
    _ffn_tile_streaming(read, write, first_step, wgu_hbm.at[layer], wd_hbm.at[layer],
                        weight_scratch, g_ref, gf_ref, a_scr, final_norm)


def _attn_out_ffn(x, y, w_o, g, wgu, wd, g_final, *, layer, w_layer, final_norm):
    bsz, seq, _ = x.shape
    tm = TOKEN_TILE
    per_class = tm // ATTN_CLASSES
    est = (_ffn_vmem_estimate(tm) + ATTN_WIDTH * D_MODEL * 2
           + 2 * tm * ATTN_WIDTH * 4
           + 3 * tm * D_MODEL * 4)
    x_tile = pl.BlockSpec((1, tm, D_MODEL), lambda b, i: (b, i, 0))
    return pl.pallas_call(
        functools.partial(_attn_out_ffn_kernel, tm=tm, layer=layer, final_norm=final_norm),
        out_shape=jax.ShapeDtypeStruct(x.shape, F32),
        grid=(bsz, seq // tm),
        in_specs=[
            x_tile,
            pl.BlockSpec((ATTN_CLASSES, 1, per_class, y.shape[-1]), lambda b, i: (0, b, i, 0)),
            _resident(w_o.shape[1:], w_layer),
            _resident((1, D_MODEL), layer),
            pl.BlockSpec(memory_space=pl.ANY),
            pl.BlockSpec(memory_space=pl.ANY),
            _resident((1, D_MODEL)),
        ],
        out_specs=x_tile,
        scratch_shapes=[pltpu.VMEM((tm, D_FF), BF16),
                        pltpu.VMEM((D_MODEL // V7X_LANES, tm, V7X_LANES), F32)]
        + _ffn_weight_scratch(),
        compiler_params=pltpu.CompilerParams(
            dimension_semantics=("arbitrary",) * 2, vmem_limit_bytes=_vmem_limit(est)),
        name="attn_out_ffn_final" if final_norm else "attn_out_ffn",
    )(x, y, w_o, g, wgu, wd, g_final)


def _conv_mixer_kernel(x_ref, g_ref, w_in_ref, ka_ref, kb_ref, bb_ref, lng_ref, lnb_ref,
                       w_out_ref, o_ref, cx_scr, u_scr, cv_scr, *, tm):
    ha, hb = CONV_A_HALO, CONV_B_HALO

    @pl.when(pl.program_id(1) == 0)
    def _start_of_sequence():
        cx_scr[0:ha, :] = jnp.zeros((ha, SC_WIDTH), F32)
        u_scr[0:hb, :] = jnp.zeros((hb, CM_WIDTH), F32)

    x = x_ref[0]
    h = _rms_norm(x, g_ref[...]).astype(BF16)

    def proj(lo, width):
        return jnp.dot(h, w_in_ref[:, lo:lo + width], preferred_element_type=F32)

    u_scr[hb:hb + tm, :] = proj(3 * SC_WIDTH, CM_WIDTH) * _sigmoid(
        proj(3 * SC_WIDTH + CM_WIDTH, CM_WIDTH))

    first_off = hb - (CM_KERNEL - 1)

    def conv_b_rows(r0):
        for j in range(CM_WIDTH // V7X_LANES):
            lanes = slice(j * V7X_LANES, (j + 1) * V7X_LANES)
            n_win = CONV_ROWS + hb
            window = u_scr[pl.ds(r0, n_win), lanes]
            acc = jnp.zeros((CONV_ROWS, V7X_LANES), F32)
            for sub in range(V7X_SUBLANES):
                offs = [o for o in range(first_off, hb + 1) if o % V7X_SUBLANES == sub]
                shifted = pltpu.roll(window, n_win - sub, axis=0) if sub else window
                for o in offs:
                    k = o - first_off
                    acc = acc + kb_ref[k:k + 1, lanes] * shifted[o - sub:o - sub + CONV_ROWS, :]
            cv_scr[pl.ds(r0, CONV_ROWS), lanes] = acc

    a_b = []

    def mixer_a_cols(lo):
        a_b.append(proj(lo, V7X_MXU_COLS))
        cx_scr[ha:ha + tm, lo:lo + V7X_MXU_COLS] = (
            proj(SC_WIDTH + lo, V7X_MXU_COLS) * proj(2 * SC_WIDTH + lo, V7X_MXU_COLS))

    def mixer_a_output():
        conv_a = ka_ref[0:1, :] * cx_scr[ha - 2:ha - 2 + tm, :]
        conv_a = conv_a + ka_ref[1:2, :] * cx_scr[ha - 1:ha - 1 + tm, :]
        conv_a = conv_a + ka_ref[2:3, :] * cx_scr[ha:ha + tm, :]
        y_a = (jnp.concatenate(a_b, axis=1) * conv_a).astype(BF16)
        return jnp.dot(y_a, w_out_ref[0:SC_WIDTH, :], preferred_element_type=F32)

    out_a = None
    for idx, r0 in enumerate(range(0, tm, CONV_ROWS)):
        conv_b_rows(r0)
        if idx < SC_WIDTH // V7X_MXU_COLS:
            mixer_a_cols(idx * V7X_MXU_COLS)
        elif out_a is None:
            out_a = mixer_a_output()

    u = cv_scr[...] + bb_ref[...]
    mu = jnp.mean(u, axis=-1, keepdims=True)
    uc = u - mu
    un = uc * lax.rsqrt(jnp.mean(uc * uc, axis=-1, keepdims=True) + LN_EPS)
    un = un * lng_ref[...] + lnb_ref[...]
    y_b = (un * _sigmoid(un)).astype(BF16)
    o_ref[0] = x + out_a + jnp.dot(y_b, w_out_ref[SC_WIDTH:SC_WIDTH + CM_WIDTH, :],
                                   preferred_element_type=F32)

    cx_scr[0:ha, :] = cx_scr[tm:tm + ha, :]
    u_scr[0:hb, :] = u_scr[tm:tm + hb, :]


def _conv_mixer(x, g, w_in, ka, kb, bb, lng, lnb, w_out, *, layer, w_layer):
    bsz, seq, _ = x.shape
    tm = TOKEN_TILE
    est = (4 * tm * D_MODEL * 4 + (D_MODEL * IN_AB + D_MODEL * D_MODEL) * 2
           + (3 * tm + 64) * SC_WIDTH * 4 + tm * D_MODEL * 2
           + 8 * tm * D_MODEL * 4)
    tile = pl.BlockSpec((1, tm, D_MODEL), lambda b, t: (b, t, 0))
    return pl.pallas_call(
        functools.partial(_conv_mixer_kernel, tm=tm),
        out_shape=jax.ShapeDtypeStruct(x.shape, F32),
        grid=(bsz, seq // tm),
        in_specs=[
            tile,
            _resident((1, D_MODEL), layer),
            _resident(w_in.shape[1:], w_layer),
            _resident(ka.shape[1:], w_layer),
            _resident(kb.shape[1:], w_layer),
            _resident((1, CM_WIDTH), w_layer),
            _resident((1, CM_WIDTH), w_layer),
            _resident((1, CM_WIDTH), w_layer),
            _resident(w_out.shape[1:], w_layer),
        ],
        out_specs=tile,
        scratch_shapes=[
            pltpu.VMEM((CONV_A_HALO + tm, SC_WIDTH), F32),
            pltpu.VMEM((CONV_B_HALO + tm, CM_WIDTH), F32),
            pltpu.VMEM((tm, CM_WIDTH), F32),
        ],
        compiler_params=pltpu.CompilerParams(
            dimension_semantics=("arbitrary", "arbitrary"),
            vmem_limit_bytes=_vmem_limit(est)),
        name="conv_mixer",
    )(x, g, w_in, ka, kb, bb, lng, lnb, w_out)


def _attention_kernel(slopes_ref, q_ref, k_ref, v_ref, o_ref, m_scr, l_scr, *, seq):
    blk, grp, ncls = ATTN_BLOCK, ATTN_GROUP, ATTN_CLASSES
    sub = blk // ncls
    pair = pl.program_id(1)
    slope = [slopes_ref[HEADS_PER_STEP * pair + h] for h in range(HEADS_PER_STEP)]
    first_head = lax.broadcasted_iota(jnp.int32, (1, V7X_LANES), 1) < HEAD_DIM
    head_lanes = [first_head, jnp.logical_not(first_head)]
    nt_dims = (((1,), (1,)), ((), ()))
    q_scale = LOG2_E / math.sqrt(HEAD_DIM)

    q_cls = [q_ref.at[c, 0] for c in range(ncls)]
    k_cls = [k_ref.at[c, 0] for c in range(ncls)]
    v_cls = [v_ref.at[c, 0] for c in range(ncls)]
    o_cls = [o_ref.at[c, 0] for c in range(ncls)]
    m_cls = [m_scr.at[c] for c in range(ncls)]
    l_cls = [l_scr.at[c] for c in range(ncls)]

    def class_rows(c, j):
        def load(refs, first, n):
            start = pl.multiple_of((j * grp + first) * blk, blk)
            return refs[c][pl.ds(start, n * blk), :]

        def store(refs, value):
            refs[c][pl.ds(pl.multiple_of(j * grp * blk, blk), grp * blk), :] = value
        return load, store

    def subclass_rows(c, a, step):
        def load(refs, first, n):
            assert first == 0 and n == grp
            return refs[c][pl.ds(a, grp * blk, stride=step), :]

        def store(refs, value):
            refs[c][pl.ds(a, grp * blk, stride=step), :] = value
        return load, store

    def natural_rows(j):
        def load(refs, first, n):
            start = pl.multiple_of((j * grp + first) * sub, sub)
            per_class = [refs[c][pl.ds(start, n * sub), :] for c in range(ncls)]
            return jnp.concatenate(
                [per_class[c][b * sub:(b + 1) * sub] for b in range(n) for c in range(ncls)], axis=0)

        def store(refs, value):
            start = pl.multiple_of(j * grp * sub, sub)
            for c in range(ncls):
                refs[c][pl.ds(start, grp * sub), :] = jnp.concatenate(
                    [value[b * blk + c * sub:b * blk + (c + 1) * sub] for b in range(grp)], axis=0)
        return load, store

    def run_group(access, first_group, bias_one, bias_two, first_branch, last_branch):
        load, store = access
        q = load(q_cls, 0, grp) * q_scale
        k_first, k_blocks = (0, grp) if first_group else (-1, grp + 1)
        kb = load(k_cls, k_first, k_blocks).astype(BF16)
        vb = load(v_cls, k_first, k_blocks).astype(BF16)
        vb = jnp.concatenate([vb, jnp.ones(vb.shape, BF16)], axis=1)
        qh = [jnp.where(head_lanes[h], q, 0.0).astype(BF16) for h in range(HEADS_PER_STEP)]

        ms, ls, pvs = [], [], []
        for u in range(grp):
            if first_group and u == 0:
                keys, bias = slice(0, blk), bias_one
            else:
                lo = u - 1 if first_group else u
                keys, bias = slice(lo * blk, (lo + 2) * blk), bias_two
            rows = slice(u * blk, (u + 1) * blk)
            lhs = jnp.concatenate([qh[h][rows] for h in range(HEADS_PER_STEP)], axis=0)
            s = lax.dot_general(lhs, kb[keys], nt_dims, preferred_element_type=F32) + bias
            m = jnp.max(s, axis=-1, keepdims=True)
            p = jnp.exp2(s - m).astype(BF16)
            r = jnp.dot(p, vb[keys], preferred_element_type=F32)
            mb = jnp.broadcast_to(m, (HEADS_PER_STEP * blk, V7X_LANES))
            ms.append(jnp.where(first_head, mb[:blk], mb[blk:]))
            pvs.append(jnp.where(first_head, r[:blk, :V7X_LANES], r[blk:, :V7X_LANES]))
            ls.append(jnp.where(first_head, r[:blk, V7X_LANES:], r[blk:, V7X_LANES:]))
        m_new = jnp.concatenate(ms, axis=0)
        l_new = jnp.concatenate(ls, axis=0)
        acc = jnp.concatenate(pvs, axis=0)

        if not first_branch:
            m_old, l_old, acc_old = load(m_cls, 0, grp), load(l_cls, 0, grp), load(o_cls, 0, grp)
            top = jnp.maximum(m_old, m_new)
            w_old = jnp.exp2(m_old - top)
            w_new = jnp.exp2(m_new - top)
            l_new = w_old * l_old + w_new * l_new
            acc = w_old * acc_old + w_new * acc
            m_new = top
        if last_branch:
            store(o_cls, acc / l_new)
        else:
            store(m_cls, m_new)
            store(l_cls, l_new)
            store(o_cls, acc)

    def bias_tables(dil, natural):
        def position(idx):
            if not natural:
                return idx
            inside = idx & (blk - 1)
            return (idx - inside) + ncls * (inside & (sub - 1)) + (inside >> int(math.log2(sub)))
        iq = position(lax.broadcasted_iota(jnp.int32, (blk, 2 * blk), 0))
        ik = position(lax.broadcasted_iota(jnp.int32, (blk, 2 * blk), 1))
        rel = blk + iq - ik
        valid = (rel >= 0) & (rel <= ATTN_BLOCK)
        dist = (dil * rel).astype(F32)
        two = jnp.concatenate(
            [jnp.where(valid, (-LOG2_E * slope[h]) * dist, NEG_INF) for h in range(HEADS_PER_STEP)],
            axis=0)
        return two[:, blk:], two

    rows_per_class = seq // ncls
    for branch, (window, dil) in enumerate(DILATED_BRANCHES):
        assert window // dil == ATTN_BLOCK
        first_branch, last_branch = branch == 0, branch == len(DILATED_BRANCHES) - 1
        bias_one, bias_two = bias_tables(dil, natural=(dil == 1))
        run = functools.partial(run_group, bias_one=bias_one, bias_two=bias_two,
                                first_branch=first_branch, last_branch=last_branch)

        def sweep(make_access, n_groups, run=run):
            run(make_access(0), True)
            if n_groups > 1:
                def later(j, carry):
                    run(make_access(j), False)
                    return carry
                lax.fori_loop(1, n_groups, later, 0, unroll=ATTN_UNROLL[dil])

        if dil == 1:
            sweep(natural_rows, seq // (grp * blk))
        elif dil == ncls:
            for c in range(ncls):
                sweep(functools.partial(class_rows, c), rows_per_class // (grp * blk))
        else:
            step = dil // ncls
            assert rows_per_class // step == grp * blk
            for c in range(ncls):
                def subclass(a, carry, c=c, step=step, run=run):
                    run(subclass_rows(c, a, step), True)
                    return carry
                lax.fori_loop(0, step, subclass, 0, unroll=ATTN_UNROLL[dil])


def _attention(slopes, qkv):
    ncls, bsz, rows, _ = qkv.shape
    n_pairs = N_HEADS // HEADS_PER_STEP
    block_bytes = ncls * rows * V7X_LANES * 4
    est = (2 * 4 * block_bytes
           + 2 * block_bytes
           + 6 * 1024 * 1024)

    def cols(offset):
        return pl.BlockSpec((ncls, 1, rows, V7X_LANES), lambda b, p: (0, b, 0, offset + p))

    return pl.pallas_call(
        functools.partial(_attention_kernel, seq=ncls * rows),
        out_shape=jax.ShapeDtypeStruct((ncls, bsz, rows, ATTN_WIDTH), F32),
        grid=(bsz, n_pairs),
        in_specs=[
            pl.BlockSpec(memory_space=pltpu.SMEM),
            cols(0), cols(n_pairs), cols(2 * n_pairs),
        ],
        out_specs=cols(0),
        scratch_shapes=[pltpu.VMEM((ncls, rows, V7X_LANES), F32),
                        pltpu.VMEM((ncls, rows, V7X_LANES), F32)],
        compiler_params=pltpu.CompilerParams(
            dimension_semantics=("arbitrary", "arbitrary"),
            vmem_limit_bytes=_vmem_limit(est)),
        name="dilated_attention",
    )(slopes, qkv, qkv, qkv)


def _alibi_slopes():
    return np.array([2.0 ** (-8.0 * (i + 1) / N_HEADS) for i in range(N_HEADS)], dtype=np.float32)


def kernel(x, ffn1_norm, ffn1_w_gate_up, ffn1_w_down, mix_norm, ffn2_norm, ffn2_w_gate_up,
           ffn2_w_down, conv_w_in, conv_a_kernel, conv_b_kernel, conv_b_bias, conv_b_ln_gain,
           conv_b_ln_bias, conv_w_out, attn_w_qkv, attn_w_o, final_norm):
    bsz, seq, d = x.shape
    n_tok = bsz * seq
    rows = lambda v: v.reshape(v.shape[0], 1, v.shape[1])
    bf16 = lambda w: w.astype(BF16)
    slopes = jnp.asarray(_alibi_slopes())
    g_final = final_norm.reshape(1, -1)
    ffn1_norm, mix_norm, ffn2_norm = rows(ffn1_norm), rows(mix_norm), rows(ffn2_norm)
    conv_b_bias, conv_b_ln_gain, conv_b_ln_bias = (
        rows(conv_b_bias), rows(conv_b_ln_gain), rows(conv_b_ln_bias))
    ffn1_wgu, ffn1_wd = ffn1_w_gate_up, ffn1_w_down
    ffn2_wgu, ffn2_wd = ffn2_w_gate_up, ffn2_w_down
    conv_w_in, conv_w_out = bf16(conv_w_in), bf16(conv_w_out)
    attn_w_qkv, attn_w_o = bf16(attn_w_qkv), bf16(attn_w_o)

    for layer in range(DEPTH):
        x = _ffn(x.reshape(n_tok, d), ffn1_norm, ffn1_wgu, ffn1_wd, g_final,
                 layer=layer, final_norm=False, down_first=(layer > 0)).reshape(bsz, seq, d)
        i = layer // 2
        last = layer == DEPTH - 1
        if layer % 2 == 0:
            x = _conv_mixer(x, mix_norm, conv_w_in, conv_a_kernel, conv_b_kernel, conv_b_bias,
                            conv_b_ln_gain, conv_b_ln_bias, conv_w_out, layer=layer, w_layer=i)
            x = _ffn(x.reshape(n_tok, d), ffn2_norm, ffn2_wgu, ffn2_wd, g_final,
                     layer=layer, final_norm=last).reshape(bsz, seq, d)
        else:
            qkv = _norm_proj(x, mix_norm, attn_w_qkv, layer=layer, w_layer=i)
            o = _attention(slopes, qkv)
            x = _attn_out_ffn(x, o, attn_w_o, ffn2_norm, ffn2_wgu, ffn2_wd, g_final,
                              layer=layer, w_layer=i, final_norm=last)
    return x
```

```python
import functools
import math

import numpy as np
import jax
import jax.numpy as jnp
from jax import lax
from jax.experimental import pallas as pl
from jax.experimental.pallas import tpu as pltpu

D_MODEL = 1024
DEPTH = 4
D_FF = 2816
RMS_EPS = 1e-6
LN_EPS = 1e-5
SC_WIDTH = 512
SC_KERNEL = 3
CM_WIDTH = 512
CM_KERNEL = 31
IN_AB = 3 * SC_WIDTH + 2 * CM_WIDTH
N_HEADS = 16
HEAD_DIM = 64
ATTN_WIDTH = N_HEADS * HEAD_DIM
DILATED_BRANCHES = ((128, 1), (512, 4), (2048, 16))
ATTN_BLOCK = 128
NEG_INF = -1e30

V7X_LANES = 128
V7X_SUBLANES = 8
V7X_MXU_COLS = 256
V7X_VMEM_BYTES = 64 * 1024 * 1024

F32 = jnp.float32
BF16 = jnp.bfloat16

TOKEN_TILE = 512
FFN_TILE = 1024
FF_CHUNK = V7X_MXU_COLS
FFN_STAGE_SLOTS = 2
CONV_ROWS = 64
CONV_A_HALO = V7X_SUBLANES
CONV_B_HALO = 32
HEADS_PER_STEP = V7X_LANES // HEAD_DIM
ATTN_GROUP = 4
ATTN_CLASSES = 4
ATTN_UNROLL = {1: 5, 4: 3, 16: 4}
LOG2_E = math.log2(math.e)


def _vmem_limit(estimate_bytes):
    return int(min(V7X_VMEM_BYTES - 8 * 1024 * 1024, estimate_bytes))


def _rms_norm(x, g):
    return x * lax.rsqrt(jnp.mean(x * x, axis=-1, keepdims=True) + RMS_EPS) * g


def _sigmoid(x):
    return 1.0 / (1.0 + jnp.exp(-x))


def _resident(shape, layer=None):
    zeros = (0,) * len(shape)
    if layer is None:
        return pl.BlockSpec(shape, lambda *_: zeros, pipeline_mode=pl.Buffered(1))
    return pl.BlockSpec((None,) + tuple(shape), lambda *_: (layer,) + zeros,
                        pipeline_mode=pl.Buffered(1))


def _ffn_tile(x, g_ref, wgu_ref, wd_ref, gf_ref, a_scr, final_norm, before_chunk=None):
    h = _rms_norm(x, g_ref[...]).astype(BF16)
    for c in range(D_FF // FF_CHUNK):
        lo = c * FF_CHUNK
        if before_chunk is not None:
            before_chunk(c)
        gate = jnp.dot(h, wgu_ref[:, lo:lo + FF_CHUNK], preferred_element_type=F32)
        up = jnp.dot(h, wgu_ref[:, D_FF + lo:D_FF + lo + FF_CHUNK], preferred_element_type=F32)
        a_scr[:, lo:lo + FF_CHUNK] = (gate * _sigmoid(gate) * up).astype(BF16)
    y = x + 0.5 * jnp.dot(a_scr[...], wd_ref[...], preferred_element_type=F32)
    if final_norm:
        y = _rms_norm(y, gf_ref[...])
    return y


def _ffn_weight_scratch():
    stage = lambda shape: pltpu.VMEM((FFN_STAGE_SLOTS,) + shape, F32)
    return [pltpu.VMEM((D_MODEL, 2 * D_FF), BF16),
            pltpu.VMEM((D_FF, D_MODEL), BF16),
            stage((D_MODEL, FF_CHUNK)), stage((D_MODEL, FF_CHUNK)), stage((FF_CHUNK, D_MODEL)),
            pltpu.SemaphoreType.DMA((FFN_STAGE_SLOTS, 3))]


def _ffn_weight_streamer(wgu_hbm, wd_hbm, scratch):
    wgu_bf, wd_bf, stage_g, stage_u, stage_d, sems = scratch
    n_chunks = D_FF // FF_CHUNK

    def parts(c):
        slot, lo = c % FFN_STAGE_SLOTS, c * FF_CHUNK
        gate_cols, up_cols = pl.ds(lo, FF_CHUNK), pl.ds(D_FF + lo, FF_CHUNK)
        down_rows = pl.ds(lo, FF_CHUNK)
        return [
            (pltpu.make_async_copy(wgu_hbm.at[:, gate_cols], stage_g.at[slot], sems.at[slot, 0]),
             stage_g, wgu_bf.at[:, gate_cols]),
            (pltpu.make_async_copy(wgu_hbm.at[:, up_cols], stage_u.at[slot], sems.at[slot, 1]),
             stage_u, wgu_bf.at[:, up_cols]),
            (pltpu.make_async_copy(wd_hbm.at[down_rows, :], stage_d.at[slot], sems.at[slot, 2]),
             stage_d, wd_bf.at[down_rows, :]),
        ]

    def fetch(c):
        for copy, _, _ in parts(c):
            copy.start()

    ahead = FFN_STAGE_SLOTS - 1

    def before_chunk(c):
        if c == 0:
            for first in range(ahead):
                fetch(first)
        if c + ahead < n_chunks:
            fetch(c + ahead)
        for copy, stage, dst in parts(c):
            copy.wait()
            dst[...] = stage[c % FFN_STAGE_SLOTS].astype(BF16)

    return before_chunk


def _ffn_tile_streaming(x_ref_tile, o_ref_tile, first_step, wgu_hbm, wd_hbm, scratch,
                        g_ref, gf_ref, a_scr, final_norm):
    wgu_bf, wd_bf = scratch[0], scratch[1]

    @pl.when(first_step)
    def _with_weight_load():
        o_ref_tile(_ffn_tile(x_ref_tile(), g_ref, wgu_bf, wd_bf, gf_ref, a_scr, final_norm,
                             before_chunk=_ffn_weight_streamer(wgu_hbm, wd_hbm, scratch)))

    @pl.when(jnp.logical_not(first_step))
    def _weights_resident():
        o_ref_tile(_ffn_tile(x_ref_tile(), g_ref, wgu_bf, wd_bf, gf_ref, a_scr, final_norm))


def _ffn_kernel(x_ref, g_ref, wgu_hbm, wd_hbm, gf_ref, o_ref, a_scr, *weight_scratch,
                layer, final_norm):
    def write(y):
        o_ref[...] = y
    _ffn_tile_streaming(lambda: x_ref[...], write, pl.program_id(0) == 0,
                        wgu_hbm.at[layer], wd_hbm.at[layer], weight_scratch,
                        g_ref, gf_ref, a_scr, final_norm)


def _ffn_vmem_estimate(tm):
    return (4 * tm * D_MODEL * 4
            + 3 * D_MODEL * D_FF * 2
            + FFN_STAGE_SLOTS * 3 * D_MODEL * FF_CHUNK * 4
            + tm * D_FF * 2
            + 6 * tm * D_MODEL * 4)


def _ffn(x2d, g, wgu, wd, g_final, *, layer, final_norm):
    n_tok = x2d.shape[0]
    tm = FFN_TILE
    est = _ffn_vmem_estimate(tm)
    return pl.pallas_call(
        functools.partial(_ffn_kernel, layer=layer, final_norm=final_norm),
        out_shape=jax.ShapeDtypeStruct((n_tok, D_MODEL), F32),
        grid=(n_tok // tm,),
        in_specs=[
            pl.BlockSpec((tm, D_MODEL), lambda i: (i, 0)),
            _resident((1, D_MODEL), layer),
            pl.BlockSpec(memory_space=pl.ANY),
            pl.BlockSpec(memory_space=pl.ANY),
            _resident((1, D_MODEL)),
        ],
        out_specs=pl.BlockSpec((tm, D_MODEL), lambda i: (i, 0)),
        scratch_shapes=[pltpu.VMEM((tm, D_FF), BF16)] + _ffn_weight_scratch(),
        compiler_params=pltpu.CompilerParams(
            dimension_semantics=("arbitrary",), vmem_limit_bytes=_vmem_limit(est)),
        name="ffn_final" if final_norm else "ffn",
    )(x2d, g, wgu, wd, g_final)


def _class_rows(c, tm):
    return pl.ds(c, tm // ATTN_CLASSES, stride=ATTN_CLASSES)


def _lane_block(j):
    return slice(j * V7X_LANES, (j + 1) * V7X_LANES)


def _to_class_major(tile, scr, tm):
    n_blocks = tile.shape[1] // V7X_LANES
    for j in range(n_blocks):
        scr[j] = tile[:, _lane_block(j)]
    return jnp.concatenate(
        [jnp.concatenate([scr[j, _class_rows(c, tm), :] for j in range(n_blocks)], axis=1)
         for c in range(ATTN_CLASSES)], axis=0)


def _from_class_major(tile, scr, tm):
    n_blocks = tile.shape[1] // V7X_LANES
    per_class = tm // ATTN_CLASSES
    for j in range(n_blocks):
        for c in range(ATTN_CLASSES):
            scr[j, _class_rows(c, tm), :] = tile[c * per_class:(c + 1) * per_class, _lane_block(j)]
    return jnp.concatenate([scr[j] for j in range(n_blocks)], axis=1)


def _norm_proj_kernel(x_ref, g_ref, w_ref, o_ref, scr, *, tm):
    per_class = tm // ATTN_CLASSES
    h = _rms_norm(_to_class_major(x_ref[0], scr, tm), g_ref[...]).astype(BF16)
    y = jnp.dot(h, w_ref[...], preferred_element_type=F32)
    for c in range(ATTN_CLASSES):
        o_ref[c, 0] = y[c * per_class:(c + 1) * per_class]


def _norm_proj(x, g, w, *, layer, w_layer):
    bsz, seq, _ = x.shape
    n_out = w.shape[-1]
    tm = TOKEN_TILE
    per_class = tm // ATTN_CLASSES
    est = (2 * tm * D_MODEL * 4 + 2 * tm * n_out * 4 + D_MODEL * n_out * 2 + 2 * tm * n_out * 4)
    return pl.pallas_call(
        functools.partial(_norm_proj_kernel, tm=tm),
        out_shape=jax.ShapeDtypeStruct((ATTN_CLASSES, bsz, seq // ATTN_CLASSES, n_out), F32),
        grid=(bsz, seq // tm),
        in_specs=[
            pl.BlockSpec((1, tm, D_MODEL), lambda b, i: (b, i, 0)),
            _resident((1, D_MODEL), layer),
            _resident(w.shape[1:], w_layer),
        ],
        out_specs=pl.BlockSpec((ATTN_CLASSES, 1, per_class, n_out), lambda b, i: (0, b, i, 0)),
        scratch_shapes=[pltpu.VMEM((D_MODEL // V7X_LANES, tm, V7X_LANES), F32)],
        compiler_params=pltpu.CompilerParams(
            dimension_semantics=("arbitrary",) * 2, vmem_limit_bytes=_vmem_limit(est)),
        name="norm_proj",
    )(x, g, w)


def _attn_out_ffn_kernel(x_ref, y_ref, wo_ref, g_ref, wgu_hbm, wd_hbm, gf_ref, o_ref,
                         a_scr, cm_scr, *weight_scratch, tm, layer, final_norm):
    first_step = (pl.program_id(0) == 0) & (pl.program_id(1) == 0)

    def read():
        y = jnp.concatenate([y_ref[c, 0] for c in range(ATTN_CLASSES)], axis=0).astype(BF16)
        r = jnp.dot(y, wo_ref[...], preferred_element_type=F32)
        return x_ref[0] + _from_class_major(r, cm_scr, tm)

    def write(y):
        o_ref[0] = y
    _ffn_tile_streaming(read, write, first_step, wgu_hbm.at[layer], wd_hbm.at[layer],
                        weight_scratch, g_ref, gf_ref, a_scr, final_norm)


def _attn_out_ffn(x, y, w_o, g, wgu, wd, g_final, *, layer, w_layer, final_norm):
    bsz, seq, _ = x.shape
    tm = TOKEN_TILE
    per_class = tm // ATTN_CLASSES
    est = (_ffn_vmem_estimate(tm) + ATTN_WIDTH * D_MODEL * 2
           + 2 * tm * ATTN_WIDTH * 4
           + 3 * tm * D_MODEL * 4)
    x_tile = pl.BlockSpec((1, tm, D_MODEL), lambda b, i: (b, i, 0))
    return pl.pallas_call(
        functools.partial(_attn_out_ffn_kernel, tm=tm, layer=layer, final_norm=final_norm),
        out_shape=jax.ShapeDtypeStruct(x.shape, F32),
        grid=(bsz, seq // tm),
        in_specs=[
            x_tile,
            pl.BlockSpec((ATTN_CLASSES, 1, per_class, y.shape[-1]), lambda b, i: (0, b, i, 0)),
            _resident(w_o.shape[1:], w_layer),
            _resident((1, D_MODEL), layer),
            pl.BlockSpec(memory_space=pl.ANY),
            pl.BlockSpec(memory_space=pl.ANY),
            _resident((1, D_MODEL)),
        ],
        out_specs=x_tile,
        scratch_shapes=[pltpu.VMEM((tm, D_FF), BF16),
                        pltpu.VMEM((D_MODEL // V7X_LANES, tm, V7X_LANES), F32)]
        + _ffn_weight_scratch(),
        compiler_params=pltpu.CompilerParams(
            dimension_semantics=("arbitrary",) * 2, vmem_limit_bytes=_vmem_limit(est)),
        name="attn_out_ffn_final" if final_norm else "attn_out_ffn",
    )(x, y, w_o, g, wgu, wd, g_final)


def _conv_mixer_kernel(x_ref, g_ref, w_in_ref, ka_ref, kb_ref, bb_ref, lng_ref, lnb_ref,
                       w_out_ref, o_ref, cx_scr, u_scr, cv_scr, *, tm):
    ha, hb = CONV_A_HALO, CONV_B_HALO

    @pl.when(pl.program_id(1) == 0)
    def _start_of_sequence():
        cx_scr[0:ha, :] = jnp.zeros((ha, SC_WIDTH), F32)
        u_scr[0:hb, :] = jnp.zeros((hb, CM_WIDTH), F32)

    x = x_ref[0]
    h = _rms_norm(x, g_ref[...]).astype(BF16)

    def proj(lo, width):
        return jnp.dot(h, w_in_ref[:, lo:lo + width], preferred_element_type=F32)

    u_scr[hb:hb + tm, :] = proj(3 * SC_WIDTH, CM_WIDTH) * _sigmoid(
        proj(3 * SC_WIDTH + CM_WIDTH, CM_WIDTH))

    first_off = hb - (CM_KERNEL - 1)

    def conv_b_rows(r0):
        for j in range(CM_WIDTH // V7X_LANES):
            lanes = slice(j * V7X_LANES, (j + 1) * V7X_LANES)
            n_win = CONV_ROWS + hb
            window = u_scr[pl.ds(r0, n_win), lanes]
            acc = jnp.zeros((CONV_ROWS, V7X_LANES), F32)
            for sub in range(V7X_SUBLANES):
                offs = [o for o in range(first_off, hb + 1) if o % V7X_SUBLANES == sub]
                shifted = pltpu.roll(window, n_win - sub, axis=0) if sub else window
                for o in offs:
                    k = o - first_off
                    acc = acc + kb_ref[k:k + 1, lanes] * shifted[o - sub:o - sub + CONV_ROWS, :]
            cv_scr[pl.ds(r0, CONV_ROWS), lanes] = acc

    a_b = []

    def mixer_a_cols(lo):
        a_b.append(proj(lo, V7X_MXU_COLS))
        cx_scr[ha:ha + tm, lo:lo + V7X_MXU_COLS] = (
            proj(SC_WIDTH + lo, V7X_MXU_COLS) * proj(2 * SC_WIDTH + lo, V7X_MXU_COLS))

    def mixer_a_output():
        conv_a = ka_ref[0:1, :] * cx_scr[ha - 2:ha - 2 + tm, :]
        conv_a = conv_a + ka_ref[1:2, :] * cx_scr[ha - 1:ha - 1 + tm, :]
        conv_a = conv_a + ka_ref[2:3, :] * cx_scr[ha:ha + tm, :]
        y_a = (jnp.concatenate(a_b, axis=1) * conv_a).astype(BF16)
        return jnp.dot(y_a, w_out_ref[0:SC_WIDTH, :], preferred_element_type=F32)

    out_a = None
    for idx, r0 in enumerate(range(0, tm, CONV_ROWS)):
        conv_b_rows(r0)
        if idx < SC_WIDTH // V7X_MXU_COLS:
            mixer_a_cols(idx * V7X_MXU_COLS)
        elif out_a is None:
            out_a = mixer_a_output()

    u = cv_scr[...] + bb_ref[...]
    mu = jnp.mean(u, axis=-1, keepdims=True)
    uc = u - mu
    un = uc * lax.rsqrt(jnp.mean(uc * uc, axis=-1, keepdims=True) + LN_EPS)
    un = un * lng_ref[...] + lnb_ref[...]
    y_b = (un * _sigmoid(un)).astype(BF16)
    o_ref[0] = x + out_a + jnp.dot(y_b, w_out_ref[SC_WIDTH:SC_WIDTH + CM_WIDTH, :],
                                   preferred_element_type=F32)

    cx_scr[0:ha, :] = cx_scr[tm:tm + ha, :]
    u_scr[0:hb, :] = u_scr[tm:tm + hb, :]


def _conv_mixer(x, g, w_in, ka, kb, bb, lng, lnb, w_out, *, layer, w_layer):
    bsz, seq, _ = x.shape
    tm = TOKEN_TILE
    est = (4 * tm * D_MODEL * 4 + (D_MODEL * IN_AB + D_MODEL * D_MODEL) * 2
           + (3 * tm + 64) * SC_WIDTH * 4 + tm * D_MODEL * 2
           + 8 * tm * D_MODEL * 4)
    tile = pl.BlockSpec((1, tm, D_MODEL), lambda b, t: (b, t, 0))
    return pl.pallas_call(
        functools.partial(_conv_mixer_kernel, tm=tm),
        out_shape=jax.ShapeDtypeStruct(x.shape, F32),
        grid=(bsz, seq // tm),
        in_specs=[
            tile,
            _resident((1, D_MODEL), layer),
            _resident(w_in.shape[1:], w_layer),
            _resident(ka.shape[1:], w_layer),
            _resident(kb.shape[1:], w_layer),
            _resident((1, CM_WIDTH), w_layer),
            _resident((1, CM_WIDTH), w_layer),
            _resident((1, CM_WIDTH), w_layer),
            _resident(w_out.shape[1:], w_layer),
        ],
        out_specs=tile,
        scratch_shapes=[
            pltpu.VMEM((CONV_A_HALO + tm, SC_WIDTH), F32),
            pltpu.VMEM((CONV_B_HALO + tm, CM_WIDTH), F32),
            pltpu.VMEM((tm, CM_WIDTH), F32),
        ],
        compiler_params=pltpu.CompilerParams(
            dimension_semantics=("arbitrary", "arbitrary"),
            vmem_limit_bytes=_vmem_limit(est)),
        name="conv_mixer",
    )(x, g, w_in, ka, kb, bb, lng, lnb, w_out)


def _attention_kernel(slopes_ref, q_ref, k_ref, v_ref, o_ref, m_scr, l_scr, *, seq):
    blk, grp, ncls = ATTN_BLOCK, ATTN_GROUP, ATTN_CLASSES
    sub = blk // ncls
    pair = pl.program_id(1)
    slope = [slopes_ref[HEADS_PER_STEP * pair + h] for h in range(HEADS_PER_STEP)]
    first_head = lax.broadcasted_iota(jnp.int32, (1, V7X_LANES), 1) < HEAD_DIM
    head_lanes = [first_head, jnp.logical_not(first_head)]
    nt_dims = (((1,), (1,)), ((), ()))
    q_scale = LOG2_E / math.sqrt(HEAD_DIM)

    q_cls = [q_ref.at[c, 0] for c in range(ncls)]
    k_cls = [k_ref.at[c, 0] for c in range(ncls)]
    v_cls = [v_ref.at[c, 0] for c in range(ncls)]
    o_cls = [o_ref.at[c, 0] for c in range(ncls)]
    m_cls = [m_scr.at[c] for c in range(ncls)]
    l_cls = [l_scr.at[c] for c in range(ncls)]

    def class_rows(c, j):
        def load(refs, first, n):
            start = pl.multiple_of((j * grp + first) * blk, blk)
            return refs[c][pl.ds(start, n * blk), :]

        def store(refs, value):
            refs[c][pl.ds(pl.multiple_of(j * grp * blk, blk), grp * blk), :] = value
        return load, store

    def subclass_rows(c, a, step):
        def load(refs, first, n):
            assert first == 0 and n == grp
            return refs[c][pl.ds(a, grp * blk, stride=step), :]

        def store(refs, value):
            refs[c][pl.ds(a, grp * blk, stride=step), :] = value
        return load, store

    def natural_rows(j):
        def load(refs, first, n):
            start = pl.multiple_of((j * grp + first) * sub, sub)
            per_class = [refs[c][pl.ds(start, n * sub), :] for c in range(ncls)]
            return jnp.concatenate(
                [per_class[c][b * sub:(b + 1) * sub] for b in range(n) for c in range(ncls)], axis=0)

        def store(refs, value):
            start = pl.multiple_of(j * grp * sub, sub)
            for c in range(ncls):
                refs[c][pl.ds(start, grp * sub), :] = jnp.concatenate(
                    [value[b * blk + c * sub:b * blk + (c + 1) * sub] for b in range(grp)], axis=0)
        return load, store

    def run_group(access, first_group, bias_one, bias_two, first_branch, last_branch):
        load, store = access
        q = load(q_cls, 0, grp) * q_scale
        k_first, k_blocks = (0, grp) if first_group else (-1, grp + 1)
        kb = load(k_cls, k_first, k_blocks).astype(BF16)
        vb = load(v_cls, k_first, k_blocks).astype(BF16)
        vb = jnp.concatenate([vb, jnp.ones(vb.shape, BF16)], axis=1)
        qh = [jnp.where(head_lanes[h], q, 0.0).astype(BF16) for h in range(HEADS_PER_STEP)]

        ms, ls, pvs = [], [], []
        for u in range(grp):
            if first_group and u == 0:
                keys, bias = slice(0, blk), bias_one
            else:
                lo = u - 1 if first_group else u
                keys, bias = slice(lo * blk, (lo + 2) * blk), bias_two
            rows = slice(u * blk, (u + 1) * blk)
            lhs = jnp.concatenate([qh[h][rows] for h in range(HEADS_PER_STEP)], axis=0)
            s = lax.dot_general(lhs, kb[keys], nt_dims, preferred_element_type=F32) + bias
            m = jnp.max(s, axis=-1, keepdims=True)
            p = jnp.exp2(s - m).astype(BF16)
            r = jnp.dot(p, vb[keys], preferred_element_type=F32)
            mb = jnp.broadcast_to(m, (HEADS_PER_STEP * blk, V7X_LANES))
            ms.append(jnp.where(first_head, mb[:blk], mb[blk:]))
            pvs.append(jnp.where(first_head, r[:blk, :V7X_LANES], r[blk:, :V7X_LANES]))
            ls.append(jnp.where(first_head, r[:blk, V7X_LANES:], r[blk:, V7X_LANES:]))
        m_new = jnp.concatenate(ms, axis=0)
        l_new = jnp.concatenate(ls, axis=0)
        acc = jnp.concatenate(pvs, axis=0)

        if not first_branch:
            m_old, l_old, acc_old = load(m_cls, 0, grp), load(l_cls, 0, grp), load(o_cls, 0, grp)
            top = jnp.maximum(m_old, m_new)
            w_old = jnp.exp2(m_old - top)
            w_new = jnp.exp2(m_new - top)
            l_new = w_old * l_old + w_new * l_new
            acc = w_old * acc_old + w_new * acc
            m_new = top
        if last_branch:
            store(o_cls, acc / l_new)
        else:
            store(m_cls, m_new)
            store(l_cls, l_new)
            store(o_cls, acc)

    def bias_tables(dil, natural):
        def position(idx):
            if not natural:
                return idx
            inside = idx & (blk - 1)
            return (idx - inside) + ncls * (inside & (sub - 1)) + (inside >> int(math.log2(sub)))
        iq = position(lax.broadcasted_iota(jnp.int32, (blk, 2 * blk), 0))
        ik = position(lax.broadcasted_iota(jnp.int32, (blk, 2 * blk), 1))
        rel = blk + iq - ik
        valid = (rel >= 0) & (rel <= ATTN_BLOCK)
        dist = (dil * rel).astype(F32)
        two = jnp.concatenate(
            [jnp.where(valid, (-LOG2_E * slope[h]) * dist, NEG_INF) for h in range(HEADS_PER_STEP)],
            axis=0)
        return two[:, blk:], two

    rows_per_class = seq // ncls
    for branch, (window, dil) in enumerate(DILATED_BRANCHES):
        assert window // dil == ATTN_BLOCK
        first_branch, last_branch = branch == 0, branch == len(DILATED_BRANCHES) - 1
        bias_one, bias_two = bias_tables(dil, natural=(dil == 1))
        run = functools.partial(run_group, bias_one=bias_one, bias_two=bias_two,
                                first_branch=first_branch, last_branch=last_branch)

        def sweep(make_access, n_groups, run=run):
            run(make_access(0), True)
            if n_groups > 1:
                def later(j, carry):
                    run(make_access(j), False)
                    return carry
                lax.fori_loop(1, n_groups, later, 0, unroll=ATTN_UNROLL[dil])

        if dil == 1:
            sweep(natural_rows, seq // (grp * blk))
        elif dil == ncls:
            for c in range(ncls):
                sweep(functools.partial(class_rows, c), rows_per_class // (grp * blk))
        else:
            step = dil // ncls
            assert rows_per_class // step == grp * blk
            for c in range(ncls):
                def subclass(a, carry, c=c, step=step, run=run):
                    run(subclass_rows(c, a, step), True)
                    return carry
                lax.fori_loop(0, step, subclass, 0, unroll=ATTN_UNROLL[dil])


def _attention(slopes, qkv):
    ncls, bsz, rows, _ = qkv.shape
    n_pairs = N_HEADS // HEADS_PER_STEP
    block_bytes = ncls * rows * V7X_LANES * 4
    est = (2 * 4 * block_bytes
           + 2 * block_bytes
           + 6 * 1024 * 1024)

    def cols(offset):
        return pl.BlockSpec((ncls, 1, rows, V7X_LANES), lambda b, p: (0, b, 0, offset + p))

    return pl.pallas_call(
        functools.partial(_attention_kernel, seq=ncls * rows),
        out_shape=jax.ShapeDtypeStruct((ncls, bsz, rows, ATTN_WIDTH), F32),
        grid=(bsz, n_pairs),
        in_specs=[
            pl.BlockSpec(memory_space=pltpu.SMEM),
            cols(0), cols(n_pairs), cols(2 * n_pairs),
        ],
        out_specs=cols(0),
        scratch_shapes=[pltpu.VMEM((ncls, rows, V7X_LANES), F32),
                        pltpu.VMEM((ncls, rows, V7X_LANES), F32)],
        compiler_params=pltpu.CompilerParams(
            dimension_semantics=("arbitrary", "arbitrary"),
            vmem_limit_bytes=_vmem_limit(est)),
        name="dilated_attention",
    )(slopes, qkv, qkv, qkv)


def _alibi_slopes():
    return np.array([2.0 ** (-8.0 * (i + 1) / N_HEADS) for i in range(N_HEADS)], dtype=np.float32)


def kernel(x, ffn1_norm, ffn1_w_gate_up, ffn1_w_down, mix_norm, ffn2_norm, ffn2_w_gate_up,
           ffn2_w_down, conv_w_in, conv_a_kernel, conv_b_kernel, conv_b_bias, conv_b_ln_gain,
           conv_b_ln_bias, conv_w_out, attn_w_qkv, attn_w_o, final_norm):
    bsz, seq, d = x.shape
    n_tok = bsz * seq
    rows = lambda v: v.reshape(v.shape[0], 1, v.shape[1])
    bf16 = lambda w: w.astype(BF16)
    slopes = jnp.asarray(_alibi_slopes())
    g_final = final_norm.reshape(1, -1)
    ffn1_norm, mix_norm, ffn2_norm = rows(ffn1_norm), rows(mix_norm), rows(ffn2_norm)
    conv_b_bias, conv_b_ln_gain, conv_b_ln_bias = (
        rows(conv_b_bias), rows(conv_b_ln_gain), rows(conv_b_ln_bias))
    ffn1_wgu, ffn1_wd = ffn1_w_gate_up, ffn1_w_down
    ffn2_wgu, ffn2_wd = ffn2_w_gate_up, ffn2_w_down
    conv_w_in, conv_w_out = bf16(conv_w_in), bf16(conv_w_out)
    attn_w_qkv, attn_w_o = bf16(attn_w_qkv), bf16(attn_w_o)

    for layer in range(DEPTH):
        x = _ffn(x.reshape(n_tok, d), ffn1_norm, ffn1_wgu, ffn1_wd, g_final,
                 layer=layer, final_norm=False).reshape(bsz, seq, d)
        i = layer // 2
        last = layer == DEPTH - 1
        if layer % 2 == 0:
            x = _conv_mixer(x, mix_norm, conv_w_in, conv_a_kernel, conv_b_kernel, conv_b_bias,
                            conv_b_ln_gain, conv_b_ln_bias, conv_w_out, layer=layer, w_layer=i)
            x = _ffn(x.reshape(n_tok, d), ffn2_norm, ffn2_wgu, ffn2_wd, g_final,
                     layer=layer, final_norm=last).reshape(bsz, seq, d)
        else:
            qkv = _norm_proj(x, mix_norm, attn_w_qkv, layer=layer, w_layer=i)
            o = _attention(slopes, qkv)
            x = _attn_out_ffn(x, o, attn_w_o, ffn2_norm, ffn2_wgu, ffn2_wd, g_final,
                              layer=layer, w_layer=i, final_norm=last)
    return x
```

```python
import functools
import math

import numpy as np
import jax
import jax.numpy as jnp
from jax import lax
from jax.experimental import pallas as pl
from jax.experimental.pallas import tpu as pltpu

D_MODEL = 1024
DEPTH = 4
D_FF = 2816
RMS_EPS = 1e-6
LN_EPS = 1e-5
SC_WIDTH = 512
SC_KERNEL = 3
CM_WIDTH = 512
CM_KERNEL = 31
IN_AB = 3 * SC_WIDTH + 2 * CM_WIDTH
N_HEADS = 16
HEAD_DIM = 64
ATTN_WIDTH = N_HEADS * HEAD_DIM
DILATED_BRANCHES = ((128, 1), (512, 4), (2048, 16))
ATTN_BLOCK = 128
NEG_INF = -1e30

V7X_LANES = 128
V7X_SUBLANES = 8
V7X_MXU_COLS = 256
V7X_VMEM_BYTES = 64 * 1024 * 1024

F32 = jnp.float32
BF16 = jnp.bfloat16

TOKEN_TILE = 512
FFN_TILE = 1024
FF_CHUNK = V7X_MXU_COLS
FFN_STAGE_SLOTS = 2
CONV_ROWS = 64
CONV_A_HALO = V7X_SUBLANES
CONV_B_HALO = 32
HEADS_PER_STEP = V7X_LANES // HEAD_DIM
ATTN_GROUP = 4
ATTN_CLASSES = 4
ATTN_UNROLL = {1: 5, 4: 3, 16: 4}
LOG2_E = math.log2(math.e)


def _vmem_limit(estimate_bytes):
    return int(min(V7X_VMEM_BYTES - 8 * 1024 * 1024, estimate_bytes))


def _rms_norm(x, g):
    return x * lax.rsqrt(jnp.mean(x * x, axis=-1, keepdims=True) + RMS_EPS) * g


def _sigmoid(x):
    return 1.0 / (1.0 + jnp.exp(-x))


def _resident(shape, layer=None):
    zeros = (0,) * len(shape)
    if layer is None:
        return pl.BlockSpec(shape, lambda *_: zeros, pipeline_mode=pl.Buffered(1))
    return pl.BlockSpec((None,) + tuple(shape), lambda *_: (layer,) + zeros,
                        pipeline_mode=pl.Buffered(1))


def _ffn_tile(x, g_ref, wgu_ref, wd_ref, gf_ref, a_scr, final_norm, before_chunk=None):
    h = _rms_norm(x, g_ref[...]).astype(BF16)
    for c in range(D_FF // FF_CHUNK):
        lo = c * FF_CHUNK
        if before_chunk is not None:
            before_chunk(c)
        gate = jnp.dot(h, wgu_ref[:, lo:lo + FF_CHUNK], preferred_element_type=F32)
        up = jnp.dot(h, wgu_ref[:, D_FF + lo:D_FF + lo + FF_CHUNK], preferred_element_type=F32)
        a_scr[:, lo:lo + FF_CHUNK] = (gate * _sigmoid(gate) * up).astype(BF16)
    y = x + 0.5 * jnp.dot(a_scr[...], wd_ref[...], preferred_element_type=F32)
    if final_norm:
        y = _rms_norm(y, gf_ref[...])
    return y


def _ffn_weight_scratch():
    stage = lambda shape: pltpu.VMEM((FFN_STAGE_SLOTS,) + shape, F32)
    return [pltpu.VMEM((D_MODEL, 2 * D_FF), BF16),
            pltpu.VMEM((D_FF, D_MODEL), BF16),
            stage((D_MODEL, FF_CHUNK)), stage((D_MODEL, FF_CHUNK)), stage((FF_CHUNK, D_MODEL)),
            pltpu.SemaphoreType.DMA((FFN_STAGE_SLOTS, 3))]


def _ffn_weight_streamer(wgu_hbm, wd_hbm, scratch):
    wgu_bf, wd_bf, stage_g, stage_u, stage_d, sems = scratch
    n_chunks = D_FF // FF_CHUNK

    def parts(c):
        slot, lo = c % FFN_STAGE_SLOTS, c * FF_CHUNK
        gate_cols, up_cols = pl.ds(lo, FF_CHUNK), pl.ds(D_FF + lo, FF_CHUNK)
        down_rows = pl.ds(lo, FF_CHUNK)
        return [
            (pltpu.make_async_copy(wgu_hbm.at[:, gate_cols], stage_g.at[slot], sems.at[slot, 0]),
             stage_g, wgu_bf.at[:, gate_cols]),
            (pltpu.make_async_copy(wgu_hbm.at[:, up_cols], stage_u.at[slot], sems.at[slot, 1]),
             stage_u, wgu_bf.at[:, up_cols]),
            (pltpu.make_async_copy(wd_hbm.at[down_rows, :], stage_d.at[slot], sems.at[slot, 2]),
             stage_d, wd_bf.at[down_rows, :]),
        ]

    def fetch(c):
        for copy, _, _ in parts(c):
            copy.start()

    ahead = FFN_STAGE_SLOTS - 1

    def before_chunk(c):
        if c == 0:
            for first in range(ahead):
                fetch(first)
        if c + ahead < n_chunks:
            fetch(c + ahead)
        for copy, stage, dst in parts(c):
            copy.wait()
            dst[...] = stage[c % FFN_STAGE_SLOTS].astype(BF16)

    return before_chunk


def _ffn_tile_streaming(x_ref_tile, o_ref_tile, first_step, wgu_hbm, wd_hbm, scratch,
                        g_ref, gf_ref, a_scr, final_norm):
    wgu_bf, wd_bf = scratch[0], scratch[1]

    @pl.when(first_step)
    def _with_weight_load():
        o_ref_tile(_ffn_tile(x_ref_tile(), g_ref, wgu_bf, wd_bf, gf_ref, a_scr, final_norm,
                             before_chunk=_ffn_weight_streamer(wgu_hbm, wd_hbm, scratch)))

    @pl.when(jnp.logical_not(first_step))
    def _weights_resident():
        o_ref_tile(_ffn_tile(x_ref_tile(), g_ref, wgu_bf, wd_bf, gf_ref, a_scr, final_norm))


def _ffn_kernel(x_ref, g_ref, wgu_hbm, wd_hbm, gf_ref, o_ref, a_scr, *weight_scratch,
                layer, final_norm):
    def write(y):
        o_ref[...] = y
    _ffn_tile_streaming(lambda: x_ref[...], write, pl.program_id(0) == 0,
                        wgu_hbm.at[layer], wd_hbm.at[layer], weight_scratch,
                        g_ref, gf_ref, a_scr, final_norm)


def _ffn_vmem_estimate(tm):
    return (4 * tm * D_MODEL * 4
            + 3 * D_MODEL * D_FF * 2
            + FFN_STAGE_SLOTS * 3 * D_MODEL * FF_CHUNK * 4
            + tm * D_FF * 2
            + 6 * tm * D_MODEL * 4)


def _ffn(x2d, g, wgu, wd, g_final, *, layer, final_norm):
    n_tok = x2d.shape[0]
    tm = FFN_TILE
    est = _ffn_vmem_estimate(tm)
    return pl.pallas_call(
        functools.partial(_ffn_kernel, layer=layer, final_norm=final_norm),
        out_shape=jax.ShapeDtypeStruct((n_tok, D_MODEL), F32),
        grid=(n_tok // tm,),
        in_specs=[
            pl.BlockSpec((tm, D_MODEL), lambda i: (i, 0)),
            _resident((1, D_MODEL), layer),
            pl.BlockSpec(memory_space=pl.ANY),
            pl.BlockSpec(memory_space=pl.ANY),
            _resident((1, D_MODEL)),
        ],
        out_specs=pl.BlockSpec((tm, D_MODEL), lambda i: (i, 0)),
        scratch_shapes=[pltpu.VMEM((tm, D_FF), BF16)] + _ffn_weight_scratch(),
        compiler_params=pltpu.CompilerParams(
            dimension_semantics=("arbitrary",), vmem_limit_bytes=_vmem_limit(est)),
        name="ffn_final" if final_norm else "ffn",
    )(x2d, g, wgu, wd, g_final)


def _class_rows(c, tm):
    return pl.ds(c, tm // ATTN_CLASSES, stride=ATTN_CLASSES)


def _lane_block(j):
    return slice(j * V7X_LANES, (j + 1) * V7X_LANES)


def _to_class_major(tile, scr, tm):
    n_blocks = tile.shape[1] // V7X_LANES
    for j in range(n_blocks):
        scr[j] = tile[:, _lane_block(j)]
    return jnp.concatenate(
        [jnp.concatenate([scr[j, _class_rows(c, tm), :] for j in range(n_blocks)], axis=1)
         for c in range(ATTN_CLASSES)], axis=0)


def _from_class_major(tile, scr, tm):
    n_blocks = tile.shape[1] // V7X_LANES
    per_class = tm // ATTN_CLASSES
    for j in range(n_blocks):
        for c in range(ATTN_CLASSES):
            scr[j, _class_rows(c, tm), :] = tile[c * per_class:(c + 1) * per_class, _lane_block(j)]
    return jnp.concatenate([scr[j] for j in range(n_blocks)], axis=1)


def _norm_proj_kernel(x_ref, g_ref, w_ref, o_ref, scr, *, tm):
    per_class = tm // ATTN_CLASSES
    h = _rms_norm(_to_class_major(x_ref[0], scr, tm), g_ref[...]).astype(BF16)
    y = jnp.dot(h, w_ref[...], preferred_element_type=F32)
    for c in range(ATTN_CLASSES):
        o_ref[c, 0] = y[c * per_class:(c + 1) * per_class]


def _norm_proj(x, g, w, *, layer, w_layer):
    bsz, seq, _ = x.shape
    n_out = w.shape[-1]
    tm = TOKEN_TILE
    per_class = tm // ATTN_CLASSES
    est = (2 * tm * D_MODEL * 4 + 2 * tm * n_out * 4 + D_MODEL * n_out * 2 + 2 * tm * n_out * 4)
    return pl.pallas_call(
        functools.partial(_norm_proj_kernel, tm=tm),
        out_shape=jax.ShapeDtypeStruct((ATTN_CLASSES, bsz, seq // ATTN_CLASSES, n_out), F32),
        grid=(bsz, seq // tm),
        in_specs=[
            pl.BlockSpec((1, tm, D_MODEL), lambda b, i: (b, i, 0)),
            _resident((1, D_MODEL), layer),
            _resident(w.shape[1:], w_layer),
        ],
        out_specs=pl.BlockSpec((ATTN_CLASSES, 1, per_class, n_out), lambda b, i: (0, b, i, 0)),
        scratch_shapes=[pltpu.VMEM((D_MODEL // V7X_LANES, tm, V7X_LANES), F32)],
        compiler_params=pltpu.CompilerParams(
            dimension_semantics=("arbitrary",) * 2, vmem_limit_bytes=_vmem_limit(est)),
        name="norm_proj",
    )(x, g, w)


def _attn_out_ffn_kernel(x_ref, y_ref, wo_ref, g_ref, wgu_hbm, wd_hbm, gf_ref, o_ref,
                         a_scr, cm_scr, *weight_scratch, tm, layer, final_norm):
    first_step = (pl.program_id(0) == 0) & (pl.program_id(1) == 0)

    def read():
        y = jnp.concatenate([y_ref[c, 0] for c in range(ATTN_CLASSES)], axis=0).astype(BF16)
        r = jnp.dot(y, wo_ref[...], preferred_element_type=F32)
        return x_ref[0] + _from_class_major(r, cm_scr, tm)

    def write(y):
        o_ref[0] = y
    _ffn_tile_streaming(read, write, first_step, wgu_hbm.at[layer], wd_hbm.at[layer],
                        weight_scratch, g_ref, gf_ref, a_scr, final_norm)


def _attn_out_ffn(x, y, w_o, g, wgu, wd, g_final, *, layer, w_layer, final_norm):
    bsz, seq, _ = x.shape
    tm = TOKEN_TILE
    per_class = tm // ATTN_CLASSES
    est = (_ffn_vmem_estimate(tm) + ATTN_WIDTH * D_MODEL * 2
           + 2 * tm * ATTN_WIDTH * 4
           + 3 * tm * D_MODEL * 4)
    x_tile = pl.BlockSpec((1, tm, D_MODEL), lambda b, i: (b, i, 0))
    return pl.pallas_call(
        functools.partial(_attn_out_ffn_kernel, tm=tm, layer=layer, final_norm=final_norm),
        out_shape=jax.ShapeDtypeStruct(x.shape, F32),
        grid=(bsz, seq // tm),
        in_specs=[
            x_tile,
            pl.BlockSpec((ATTN_CLASSES, 1, per_class, y.shape[-1]), lambda b, i: (0, b, i, 0)),
            _resident(w_o.shape[1:], w_layer),
            _resident((1, D_MODEL), layer),
            pl.BlockSpec(memory_space=pl.ANY),
            pl.BlockSpec(memory_space=pl.ANY),
            _resident((1, D_MODEL)),
        ],
        out_specs=x_tile,
        scratch_shapes=[pltpu.VMEM((tm, D_FF), BF16),
                        pltpu.VMEM((D_MODEL // V7X_LANES, tm, V7X_LANES), F32)]
        + _ffn_weight_scratch(),
        compiler_params=pltpu.CompilerParams(
            dimension_semantics=("arbitrary",) * 2, vmem_limit_bytes=_vmem_limit(est)),
        name="attn_out_ffn_final" if final_norm else "attn_out_ffn",
    )(x, y, w_o, g, wgu, wd, g_final)


def _conv_mixer_kernel(x_ref, g_ref, w_in_ref, ka_ref, kb_ref, bb_ref, lng_ref, lnb_ref,
                       w_out_ref, o_ref, cx_scr, u_scr, cv_scr, *, tm):
    ha, hb = CONV_A_HALO, CONV_B_HALO

    @pl.when(pl.program_id(1) == 0)
    def _start_of_sequence():
        cx_scr[0:ha, :] = jnp.zeros((ha, SC_WIDTH), F32)
        u_scr[0:hb, :] = jnp.zeros((hb, CM_WIDTH), F32)

    x = x_ref[0]
    h = _rms_norm(x, g_ref[...]).astype(BF16)

    def proj(lo, width):
        return jnp.dot(h, w_in_ref[:, lo:lo + width], preferred_element_type=F32)

    u_scr[hb:hb + tm, :] = proj(3 * SC_WIDTH, CM_WIDTH) * _sigmoid(
        proj(3 * SC_WIDTH + CM_WIDTH, CM_WIDTH))

    first_off = hb - (CM_KERNEL - 1)

    def conv_b_rows(r0):
        for j in range(CM_WIDTH // V7X_LANES):
            lanes = slice(j * V7X_LANES, (j + 1) * V7X_LANES)
            n_win = CONV_ROWS + hb
            window = u_scr[pl.ds(r0, n_win), lanes]
            acc = jnp.zeros((CONV_ROWS, V7X_LANES), F32)
            for sub in range(V7X_SUBLANES):
                offs = [o for o in range(first_off, hb + 1) if o % V7X_SUBLANES == sub]
                shifted = pltpu.roll(window, n_win - sub, axis=0) if sub else window
                for o in offs:
                    k = o - first_off
                    acc = acc + kb_ref[k:k + 1, lanes] * shifted[o - sub:o - sub + CONV_ROWS, :]
            cv_scr[pl.ds(r0, CONV_ROWS), lanes] = acc

    a_b = []

    def mixer_a_cols(lo):
        a_b.append(proj(lo, V7X_MXU_COLS))
        cx_scr[ha:ha + tm, lo:lo + V7X_MXU_COLS] = (
            proj(SC_WIDTH + lo, V7X_MXU_COLS) * proj(2 * SC_WIDTH + lo, V7X_MXU_COLS))

    def mixer_a_output():
        conv_a = ka_ref[0:1, :] * cx_scr[ha - 2:ha - 2 + tm, :]
        conv_a = conv_a + ka_ref[1:2, :] * cx_scr[ha - 1:ha - 1 + tm, :]
        conv_a = conv_a + ka_ref[2:3, :] * cx_scr[ha:ha + tm, :]
        y_a = (jnp.concatenate(a_b, axis=1) * conv_a).astype(BF16)
        return jnp.dot(y_a, w_out_ref[0:SC_WIDTH, :], preferred_element_type=F32)

    out_a = None
    for idx, r0 in enumerate(range(0, tm, CONV_ROWS)):
        conv_b_rows(r0)
        if idx < SC_WIDTH // V7X_MXU_COLS:
            mixer_a_cols(idx * V7X_MXU_COLS)
        elif out_a is None:
            out_a = mixer_a_output()

    u = cv_scr[...] + bb_ref[...]
    mu = jnp.mean(u, axis=-1, keepdims=True)
    uc = u - mu
    un = uc * lax.rsqrt(jnp.mean(uc * uc, axis=-1, keepdims=True) + LN_EPS)
    un = un * lng_ref[...] + lnb_ref[...]
    y_b = (un * _sigmoid(un)).astype(BF16)
    o_ref[0] = x + out_a + jnp.dot(y_b, w_out_ref[SC_WIDTH:SC_WIDTH + CM_WIDTH, :],
                                   preferred_element_type=F32)

    cx_scr[0:ha, :] = cx_scr[tm:tm + ha, :]
    u_scr[0:hb, :] = u_scr[tm:tm + hb, :]


def _conv_mixer(x, g, w_in, ka, kb, bb, lng, lnb, w_out, *, layer, w_layer):
    bsz, seq, _ = x.shape
    tm = TOKEN_TILE
    est = (4 * tm * D_MODEL * 4 + (D_MODEL * IN_AB + D_MODEL * D_MODEL) * 2
           + (3 * tm + 64) * SC_WIDTH * 4 + tm * D_MODEL * 2
           + 8 * tm * D_MODEL * 4)
    tile = pl.BlockSpec((1, tm, D_MODEL), lambda b, t: (b, t, 0))
    return pl.pallas_call(
        functools.partial(_conv_mixer_kernel, tm=tm),
        out_shape=jax.ShapeDtypeStruct(x.shape, F32),
        grid=(bsz, seq // tm),
        in_specs=[
            tile,
            _resident((1, D_MODEL), layer),
            _resident(w_in.shape[1:], w_layer),
            _resident(ka.shape[1:], w_layer),
            _resident(kb.shape[1:], w_layer),
            _resident((1, CM_WIDTH), w_layer),
            _resident((1, CM_WIDTH), w_layer),
            _resident((1, CM_WIDTH), w_layer),
            _resident(w_out.shape[1:], w_layer),
        ],
        out_specs=tile,
        scratch_shapes=[
            pltpu.VMEM((CONV_A_HALO + tm, SC_WIDTH), F32),
            pltpu.VMEM((CONV_B_HALO + tm, CM_WIDTH), F32),
            pltpu.VMEM((tm, CM_WIDTH), F32),
        ],
        compiler_params=pltpu.CompilerParams(
            dimension_semantics=("arbitrary", "arbitrary"),
            vmem_limit_bytes=_vmem_limit(est)),
        name="conv_mixer",
    )(x, g, w_in, ka, kb, bb, lng, lnb, w_out)


def _attention_kernel(slopes_ref, q_ref, k_ref, v_ref, o_ref, m_scr, l_scr, *, seq):
    blk, grp, ncls = ATTN_BLOCK, ATTN_GROUP, ATTN_CLASSES
    sub = blk // ncls
    pair = pl.program_id(1)
    slope = [slopes_ref[HEADS_PER_STEP * pair + h] for h in range(HEADS_PER_STEP)]
    first_head = lax.broadcasted_iota(jnp.int32, (1, V7X_LANES), 1) < HEAD_DIM
    head_lanes = [first_head, jnp.logical_not(first_head)]
    nt_dims = (((1,), (1,)), ((), ()))
    q_scale = LOG2_E / math.sqrt(HEAD_DIM)

    q_cls = [q_ref.at[c, 0] for c in range(ncls)]
    k_cls = [k_ref.at[c, 0] for c in range(ncls)]
    v_cls = [v_ref.at[c, 0] for c in range(ncls)]
    o_cls = [o_ref.at[c, 0] for c in range(ncls)]
    m_cls = [m_scr.at[c] for c in range(ncls)]
    l_cls = [l_scr.at[c] for c in range(ncls)]

    def class_rows(c, j):
        def load(refs, first, n):
            start = pl.multiple_of((j * grp + first) * blk, blk)
            return refs[c][pl.ds(start, n * blk), :]

        def store(refs, value):
            refs[c][pl.ds(pl.multiple_of(j * grp * blk, blk), grp * blk), :] = value
        return load, store

    def subclass_rows(c, a, step):
        def load(refs, first, n):
            assert first == 0 and n == grp
            return refs[c][pl.ds(a, grp * blk, stride=step), :]

        def store(refs, value):
            refs[c][pl.ds(a, grp * blk, stride=step), :] = value
        return load, store

    def natural_rows(j):
        def load(refs, first, n):
            start = pl.multiple_of((j * grp + first) * sub, sub)
            per_class = [refs[c][pl.ds(start, n * sub), :] for c in range(ncls)]
            return jnp.concatenate(
                [per_class[c][b * sub:(b + 1) * sub] for b in range(n) for c in range(ncls)], axis=0)

        def store(refs, value):
            start = pl.multiple_of(j * grp * sub, sub)
            for c in range(ncls):
                refs[c][pl.ds(start, grp * sub), :] = jnp.concatenate(
                    [value[b * blk + c * sub:b * blk + (c + 1) * sub] for b in range(grp)], axis=0)
        return load, store

    def run_group(access, first_group, bias_one, bias_two, first_branch, last_branch):
        load, store = access
        q = load(q_cls, 0, grp) * q_scale
        k_first, k_blocks = (0, grp) if first_group else (-1, grp + 1)
        kb = load(k_cls, k_first, k_blocks).astype(BF16)
        vb = load(v_cls, k_first, k_blocks).astype(BF16)
        vb = jnp.concatenate([vb, jnp.ones(vb.shape, BF16)], axis=1)
        qh = [jnp.where(head_lanes[h], q, 0.0).astype(BF16) for h in range(HEADS_PER_STEP)]

        ms, ls, pvs = [], [], []
        for u in range(grp):
            if first_group and u == 0:
                keys, bias = slice(0, blk), bias_one
            else:
                lo = u - 1 if first_group else u
                keys, bias = slice(lo * blk, (lo + 2) * blk), bias_two
            rows = slice(u * blk, (u + 1) * blk)
            per_head = []
            for h in range(HEADS_PER_STEP):
                s = lax.dot_general(qh[h][rows], kb[keys], nt_dims,
                                    preferred_element_type=F32) + bias[h * blk:(h + 1) * blk]
                m = jnp.max(s, axis=-1, keepdims=True)
                p = jnp.exp2(s - m).astype(BF16)
                r = jnp.dot(p, vb[keys], preferred_element_type=F32)
                per_head.append((jnp.broadcast_to(m, (blk, V7X_LANES)), r))
            (m0, r0), (m1, r1) = per_head
            ms.append(jnp.where(first_head, m0, m1))
            pvs.append(jnp.where(first_head, r0[:, :V7X_LANES], r1[:, :V7X_LANES]))
            ls.append(jnp.where(first_head, r0[:, V7X_LANES:], r1[:, V7X_LANES:]))
        m_new = jnp.concatenate(ms, axis=0)
        l_new = jnp.concatenate(ls, axis=0)
        acc = jnp.concatenate(pvs, axis=0)

        if not first_branch:
            m_old, l_old, acc_old = load(m_cls, 0, grp), load(l_cls, 0, grp), load(o_cls, 0, grp)
            top = jnp.maximum(m_old, m_new)
            w_old = jnp.exp2(m_old - top)
            w_new = jnp.exp2(m_new - top)
            l_new = w_old * l_old + w_new * l_new
            acc = w_old * acc_old + w_new * acc
            m_new = top
        if last_branch:
            store(o_cls, acc / l_new)
        else:
            store(m_cls, m_new)
            store(l_cls, l_new)
            store(o_cls, acc)

    def bias_tables(dil, natural):
        def position(idx):
            if not natural:
                return idx
            inside = idx & (blk - 1)
            return (idx - inside) + ncls * (inside & (sub - 1)) + (inside >> int(math.log2(sub)))
        iq = position(lax.broadcasted_iota(jnp.int32, (blk, 2 * blk), 0))
        ik = position(lax.broadcasted_iota(jnp.int32, (blk, 2 * blk), 1))
        rel = blk + iq - ik
        valid = (rel >= 0) & (rel <= ATTN_BLOCK)
        dist = (dil * rel).astype(F32)
        two = jnp.concatenate(
            [jnp.where(valid, (-LOG2_E * slope[h]) * dist, NEG_INF) for h in range(HEADS_PER_STEP)],
            axis=0)
        return two[:, blk:], two

    rows_per_class = seq // ncls
    for branch, (window, dil) in enumerate(DILATED_BRANCHES):
        assert window // dil == ATTN_BLOCK
        first_branch, last_branch = branch == 0, branch == len(DILATED_BRANCHES) - 1
        bias_one, bias_two = bias_tables(dil, natural=(dil == 1))
        run = functools.partial(run_group, bias_one=bias_one, bias_two=bias_two,
                                first_branch=first_branch, last_branch=last_branch)

        def sweep(make_access, n_groups, run=run):
            run(make_access(0), True)
            if n_groups > 1:
                def later(j, carry):
                    run(make_access(j), False)
                    return carry
                lax.fori_loop(1, n_groups, later, 0, unroll=ATTN_UNROLL[dil])

        if dil == 1:
            sweep(natural_rows, seq // (grp * blk))
        elif dil == ncls:
            for c in range(ncls):
                sweep(functools.partial(class_rows, c), rows_per_class // (grp * blk))
        else:
            step = dil // ncls
            assert rows_per_class // step == grp * blk
            for c in range(ncls):
                def subclass(a, carry, c=c, step=step, run=run):
                    run(subclass_rows(c, a, step), True)
                    return carry
                lax.fori_loop(0, step, subclass, 0, unroll=ATTN_UNROLL[dil])


def _attention(slopes, qkv):
    ncls, bsz, rows, _ = qkv.shape
    n_pairs = N_HEADS // HEADS_PER_STEP
    block_bytes = ncls * rows * V7X_LANES * 4
    est = (2 * 4 * block_bytes
           + 2 * block_bytes
           + 6 * 1024 * 1024)

    def cols(offset):
        return pl.BlockSpec((ncls, 1, rows, V7X_LANES), lambda b, p: (0, b, 0, offset + p))

    return pl.pallas_call(
        functools.partial(_attention_kernel, seq=ncls * rows),
        out_shape=jax.ShapeDtypeStruct((ncls, bsz, rows, ATTN_WIDTH), F32),
        grid=(bsz, n_pairs),
        in_specs=[
            pl.BlockSpec(memory_space=pltpu.SMEM),
            cols(0), cols(n_pairs), cols(2 * n_pairs),
        ],
        out_specs=cols(0),
        scratch_shapes=[pltpu.VMEM((ncls, rows, V7X_LANES), F32),
                        pltpu.VMEM((ncls, rows, V7X_LANES), F32)],
        compiler_params=pltpu.CompilerParams(
            dimension_semantics=("arbitrary", "arbitrary"),
            vmem_limit_bytes=_vmem_limit(est)),
        name="dilated_attention",
    )(slopes, qkv, qkv, qkv)


def _alibi_slopes():
    return np.array([2.0 ** (-8.0 * (i + 1) / N_HEADS) for i in range(N_HEADS)], dtype=np.float32)


def kernel(x, ffn1_norm, ffn1_w_gate_up, ffn1_w_down, mix_norm, ffn2_norm, ffn2_w_gate_up,
           ffn2_w_down, conv_w_in, conv_a_kernel, conv_b_kernel, conv_b_bias, conv_b_ln_gain,
           conv_b_ln_bias, conv_w_out, attn_w_qkv, attn_w_o, final_norm):
    bsz, seq, d = x.shape
    n_tok = bsz * seq
    rows = lambda v: v.reshape(v.shape[0], 1, v.shape[1])
    bf16 = lambda w: w.astype(BF16)
    slopes = jnp.asarray(_alibi_slopes())
    g_final = final_norm.reshape(1, -1)
    ffn1_norm, mix_norm, ffn2_norm = rows(ffn1_norm), rows(mix_norm), rows(ffn2_norm)
    conv_b_bias, conv_b_ln_gain, conv_b_ln_bias = (
        rows(conv_b_bias), rows(conv_b_ln_gain), rows(conv_b_ln_bias))
    ffn1_wgu, ffn1_wd = ffn1_w_gate_up, ffn1_w_down
    ffn2_wgu, ffn2_wd = ffn2_w_gate_up, ffn2_w_down
    conv_w_in, conv_w_out = bf16(conv_w_in), bf16(conv_w_out)
    attn_w_qkv, attn_w_o = bf16(attn_w_qkv), bf16(attn_w_o)

    for layer in range(DEPTH):
        x = _ffn(x.reshape(n_tok, d), ffn1_norm, ffn1_wgu, ffn1_wd, g_final,
                 layer=layer, final_norm=False).reshape(bsz, seq, d)
        i = layer // 2
        last = layer == DEPTH - 1
        if layer % 2 == 0:
            x = _conv_mixer(x, mix_norm, conv_w_in, conv_a_kernel, conv_b_kernel, conv_b_bias,
                            conv_b_ln_gain, conv_b_ln_bias, conv_w_out, layer=layer, w_layer=i)
            x = _ffn(x.reshape(n_tok, d), ffn2_norm, ffn2_wgu, ffn2_wd, g_final,
                     layer=layer, final_norm=last).reshape(bsz, seq, d)
        else:
            qkv = _norm_proj(x, mix_norm, attn_w_qkv, layer=layer, w_layer=i)
            o = _attention(slopes, qkv)
            x = _attn_out_ffn(x, o, attn_w_o, ffn2_norm, ffn2_wgu, ffn2_wd, g_final,
                              layer=layer, w_layer=i, final_norm=last)
    return x
```
